```python
import math
import jax, jax.numpy as jnp
from jax import lax
import numpy as np

D_MODEL = 2048
BATCH = 4
SEQ = 4096
DEPTH = 2

GRID_W = 64
CTX_LEN = 256
N_MIXERS = 2
N_ATTN_LAYERS = (DEPTH + N_MIXERS - 1) // N_MIXERS
N_FOURIER_LAYERS = DEPTH // N_MIXERS
HEAD_DIM = 64
VALUE_DIM = 2 * HEAD_DIM
N_HEADS = D_MODEL // VALUE_DIM
Q_WIDTH = N_HEADS * 2 * HEAD_DIM
ATTN_WIDTH = N_HEADS * VALUE_DIM
ATTN_SCALE = HEAD_DIM ** -0.5
Q_BLOCK = 128
ROPE_THETA = 10000.0
ROPE_FREQS = HEAD_DIM // 4
FOURIER_GROUPS = 8
FOURIER_WIDTH = D_MODEL
FOURIER_GROUP_DIM = FOURIER_WIDTH // FOURIER_GROUPS
FFN_HIDDEN = ((8 * D_MODEL // 3 + 255) // 256) * 256
RMS_EPS = 1e-6

kernel_name = 'hybrid_diffattn_fnet_dit_trunk'


def rms_norm(x, g):
    xf = x.astype(jnp.float32)
    y = xf * lax.rsqrt(jnp.mean(xf * xf, axis=-1, keepdims=True) + RMS_EPS)
    return (y * g.astype(jnp.float32)).astype(x.dtype)


def ada_params(cond, w_mod, b_mod):
    m = jax.nn.silu(cond) @ w_mod + b_mod
    return jnp.split(m[..., None, :], 6, axis=-1)


def modulate(h, shift, scale):
    return h * (1.0 + scale) + shift


def axial_rope_tables(n_tokens):
    rows_count = n_tokens // GRID_W
    row = jnp.repeat(jnp.arange(rows_count), GRID_W)
    col = jnp.tile(jnp.arange(GRID_W), rows_count)
    inv_freq = ROPE_THETA ** (-jnp.arange(ROPE_FREQS, dtype=jnp.float32) / ROPE_FREQS)
    ang = jnp.stack([row[:, None] * inv_freq, col[:, None] * inv_freq], axis=1)
    return jnp.cos(ang), jnp.sin(ang)


def apply_axial_rope(x, cos, sin):
    xs = x.reshape(x.shape[:-1] + (2, 2, ROPE_FREQS))
    xa, xb = xs[..., 0, :], xs[..., 1, :]
    c = cos[None, :, None, None].astype(x.dtype)
    s = sin[None, :, None, None].astype(x.dtype)
    out = jnp.stack([xa * c - xb * s, xb * c + xa * s], axis=-2)
    return out.reshape(x.shape)


def diff_attention(h_lat, h_ctx, w_qkv, w_o, lam, subln_g, lambda_init, cos, sin, with_ctx_queries):
    B, N, _ = h_lat.shape
    Lc = h_ctx.shape[1]
    lf = lam.astype(jnp.float32)
    lam_full = jnp.exp(jnp.sum(lf[0] * lf[1])) - jnp.exp(jnp.sum(lf[2] * lf[3])) + lambda_init

    q_l, k_l, v_l = jnp.split(h_lat @ w_qkv, [Q_WIDTH, 2 * Q_WIDTH], axis=-1)
    q_l = apply_axial_rope(q_l.reshape(B, N, N_HEADS, 2, HEAD_DIM), cos, sin)
    k_l = apply_axial_rope(k_l.reshape(B, N, N_HEADS, 2, HEAD_DIM), cos, sin)
    v_l = v_l.reshape(B, N, N_HEADS, VALUE_DIM)
    k_c, v_c = jnp.split(h_ctx @ w_qkv[:, Q_WIDTH:], 2, axis=-1)
    k_c = k_c.reshape(B, Lc, N_HEADS, 2, HEAD_DIM)
    v_c = v_c.reshape(B, Lc, N_HEADS, VALUE_DIM)
    k_all = jnp.concatenate([k_l, k_c], axis=1)
    v_all = jnp.concatenate([v_l, v_c], axis=1)

    def attend(q, k, v):
        s = jnp.einsum('bqhcd,bkhcd->bhcqk', q, k).astype(jnp.float32) * ATTN_SCALE
        p = jax.nn.softmax(s, axis=-1)
        a = (p[:, :, 0] - lam_full * p[:, :, 1]).astype(v.dtype)
        o = jnp.einsum('bhqk,bkhe->bqhe', a, v)
        o = rms_norm(o, subln_g) * (1.0 - lambda_init)
        return o.reshape(o.shape[0], o.shape[1], ATTN_WIDTH)

    n_blocks = N // Q_BLOCK
    q_blocks = q_l.reshape(B, n_blocks, Q_BLOCK, N_HEADS, 2, HEAD_DIM).swapaxes(0, 1)
    o_l = lax.map(lambda qb: attend(qb, k_all, v_all), q_blocks)
    y_l = o_l.swapaxes(0, 1).reshape(B, N, ATTN_WIDTH) @ w_o
    y_c = None
    if with_ctx_queries:
        q_c = (h_ctx @ w_qkv[:, :Q_WIDTH]).reshape(B, Lc, N_HEADS, 2, HEAD_DIM)
        y_c = attend(q_c, k_c, v_c) @ w_o
    return y_l, y_c


def fourier_mix(h, w_in, w_out):
    B, N, _ = h.shape
    u = (h @ w_in).reshape(B, N, FOURIER_GROUPS, FOURIER_GROUP_DIM).astype(jnp.float32)
    f = jnp.fft.fft2(u, axes=(1, 3), norm='ortho').real
    return f.astype(h.dtype).reshape(B, N, FOURIER_WIDTH) @ w_out


def swiglu(h, w_gu, w_down):
    g, u = jnp.split(h @ w_gu, 2, axis=-1)
    return (jax.nn.silu(g) * u) @ w_down


def setup_inputs(seed: int = 0) -> dict:
    key = jax.random.key(seed)
    ks = jax.random.split(key, 16)
    nrm = jax.random.normal
    D = D_MODEL
    return {
        'x': nrm(ks[0], (BATCH, SEQ, D), jnp.float32),
        'c': nrm(ks[1], (BATCH, D), jnp.float32),
        'ctx': nrm(ks[2], (BATCH, CTX_LEN, D), jnp.float32),
        'c_ctx': nrm(ks[3], (D,), jnp.float32),
        'mod_w': nrm(ks[4], (DEPTH, D, 6 * D), jnp.float32) * (0.5 * D ** -0.5),
        'mod_b': 0.01 * nrm(ks[5], (DEPTH, 6 * D), jnp.float32),
        'norm_g': 1.0 + 0.05 * nrm(ks[6], (DEPTH, 4, D), jnp.float32),
        'ffn_w_gu': nrm(ks[7], (DEPTH, D, 2 * FFN_HIDDEN), jnp.float32) * D ** -0.5,
        'ffn_w_down': nrm(ks[8], (DEPTH, FFN_HIDDEN, D), jnp.float32) * FFN_HIDDEN ** -0.5,
        'attn_w_qkv': nrm(ks[9], (N_ATTN_LAYERS, D, 2 * Q_WIDTH + ATTN_WIDTH), jnp.float32) * D ** -0.5,
        'attn_w_o': nrm(ks[10], (N_ATTN_LAYERS, ATTN_WIDTH, D), jnp.float32) * ATTN_WIDTH ** -0.5,
        'attn_lambda': 0.1 * nrm(ks[11], (N_ATTN_LAYERS, 4, HEAD_DIM), jnp.float32),
        'attn_subln_g': 1.0 + 0.05 * nrm(ks[12], (N_ATTN_LAYERS, VALUE_DIM), jnp.float32),
        'four_w_in': nrm(ks[13], (N_FOURIER_LAYERS, D, FOURIER_WIDTH), jnp.float32) * D ** -0.5,
        'four_w_out': nrm(ks[14], (N_FOURIER_LAYERS, FOURIER_WIDTH, D), jnp.float32) * FOURIER_WIDTH ** -0.5,
    }


def reference(x, c, ctx, c_ctx, mod_w, mod_b, norm_g, ffn_w_gu, ffn_w_down, attn_w_qkv, attn_w_o,
              attn_lambda, attn_subln_g, four_w_in, four_w_out):
    n_tokens = x.shape[1]
    cos, sin = axial_rope_tables(n_tokens)
    x_c = ctx
    last_ctx_reader = max(i for i in range(DEPTH) if i % N_MIXERS == 0)
    for i in range(DEPTH):
        j = i // N_MIXERS
        is_attn = (i % N_MIXERS) == 0
        ctx_update = i < last_ctx_reader
        need_h_c = ctx_update or (is_attn and i <= last_ctx_reader)
        g = norm_g[i]
        sh1, sc1, gt1, sh2, sc2, gt2 = ada_params(c, mod_w[i], mod_b[i])
        h_l = modulate(rms_norm(x, g[0]), sh1, sc1)
        if need_h_c:
            csh1, csc1, cgt1, csh2, csc2, cgt2 = ada_params(c_ctx, mod_w[i], mod_b[i])
            h_c = modulate(rms_norm(x_c, g[0]), csh1, csc1)
        if is_attn:
            lambda_init = 0.8 - 0.6 * math.exp(-0.3 * i)
            y_l, y_c = diff_attention(h_l, h_c, attn_w_qkv[j], attn_w_o[j], attn_lambda[j], attn_subln_g[j],
                                      lambda_init, cos, sin, ctx_update)
        else:
            y_l = fourier_mix(h_l, four_w_in[j], four_w_out[j])
            y_c = fourier_mix(h_c, four_w_in[j], four_w_out[j]) if ctx_update else None
        x = x + gt1 * rms_norm(y_l, g[1])
        x = x + gt2 * rms_norm(swiglu(modulate(rms_norm(x, g[2]), sh2, sc2), ffn_w_gu[i], ffn_w_down[i]), g[3])
        if ctx_update:
            x_c = x_c + cgt1 * rms_norm(y_c, g[1])
            x_c = x_c + cgt2 * rms_norm(swiglu(modulate(rms_norm(x_c, g[2]), csh2, csc2), ffn_w_gu[i], ffn_w_down[i]), g[3])
    return x
```

```python
import functools
import math

import jax
import jax.numpy as jnp
from jax import lax
from jax.experimental import pallas as pl
from jax.experimental.pallas import tpu as pltpu

D = 2048
HEAD_DIM = 64
VALUE_DIM = 2 * HEAD_DIM
N_HEADS = D // VALUE_DIM
GRID_W = 64
ROPE_FREQS = HEAD_DIM // 4
ROPE_THETA = 10000.0
GROUPS = 8
GROUP_DIM = D // GROUPS
FFN = ((8 * D // 3 + 255) // 256) * 256
EPS = 1e-6
LANES = 128
VMEM_LIMIT = 56 * 1024 * 1024

F32 = jnp.float32
BF16 = jnp.bfloat16


def _cparams(sem):
    return pltpu.CompilerParams(dimension_semantics=sem, vmem_limit_bytes=VMEM_LIMIT)


def _rms(x, g):
    return x * lax.rsqrt(jnp.mean(x * x, axis=-1, keepdims=True) + EPS) * g


def _mm(a, b):
    return jnp.dot(a, b, preferred_element_type=F32)


def _ada_kernel(c_ref, w_ref, b_ref, o_ref):
    c = c_ref[...]
    s = (c * jax.nn.sigmoid(c)).astype(BF16)
    o_ref[0] = _mm(s, w_ref[0].astype(BF16)) + b_ref[0]


def _ada_call(cond, mod_w, mod_b):
    depth = mod_w.shape[0]
    tn = 1024
    return pl.pallas_call(
        _ada_kernel,
        grid=(depth, 6 * D // tn),
        in_specs=[
            pl.BlockSpec((8, D), lambda l, n: (0, 0)),
            pl.BlockSpec((1, D, tn), lambda l, n: (l, 0, n)),
            pl.BlockSpec((1, 1, tn), lambda l, n: (l, 0, n)),
        ],
        out_specs=pl.BlockSpec((1, 8, tn), lambda l, n: (l, 0, n)),
        out_shape=jax.ShapeDtypeStruct((depth, 8, 6 * D), F32),
        compiler_params=_cparams(("parallel", "parallel")),
        name="ada",
    )(cond, mod_w, mod_b.reshape(depth, 1, 6 * D))


def _rope(t, c, s_lo, s_hi):
    return t * c + pltpu.roll(t, 16, 1) * s_hi + pltpu.roll(t, LANES - 16, 1) * s_lo


def _qkv_kernel(x_ref, sh_ref, sc_ref, g_ref, wq_ref, wk_ref, wv_ref, c_ref, slo_ref, shi_ref,
                q_ref, k_ref, v_ref, h_scr, *, heads_per_step):
    @pl.when(pl.program_id(2) == 0)
    def _():
        h = _rms(x_ref[0], g_ref[...]) * (1.0 + sc_ref[0]) + sh_ref[0]
        h_scr[...] = h.astype(BF16)

    h = h_scr[...]
    q = _mm(h, wq_ref[...])
    k = _mm(h, wk_ref[...])
    v = _mm(h, wv_ref[...])
    c, s_lo, s_hi = c_ref[...], slo_ref[...], shi_ref[...]
    for j in range(heads_per_step):
        sl = slice(j * LANES, (j + 1) * LANES)
        q_ref[0, j] = (_rope(q[:, sl], c, s_lo, s_hi) * (HEAD_DIM ** -0.5)).astype(BF16)
        k_ref[0, j] = _rope(k[:, sl], c, s_lo, s_hi).astype(BF16)
        v_ref[0, j] = v[:, sl].astype(BF16)


def _qkv_call(x, sh, sc, g, w_qkv, rope_c, rope_slo, rope_shi, n_keys):
    B, N, _ = x.shape
    tm, tn = 512, 512
    hps = tn // LANES
    n_steps = D // tn
    kv_shape = jax.ShapeDtypeStruct((B, N_HEADS, n_keys, VALUE_DIM), BF16)
    tab = pl.BlockSpec((tm, LANES), lambda b, m, n: (m, 0))
    vec = pl.BlockSpec((1, 1, D), lambda b, m, n: (b, 0, 0))
    out = pl.BlockSpec((1, hps, tm, LANES), lambda b, m, n: (b, n, m, 0))
    return pl.pallas_call(
        functools.partial(_qkv_kernel, heads_per_step=hps),
        grid=(B, N // tm, n_steps),
        in_specs=[
            pl.BlockSpec((1, tm, D), lambda b, m, n: (b, m, 0)),
            vec, vec,
            pl.BlockSpec((1, D), lambda b, m, n: (0, 0)),
            pl.BlockSpec((D, tn), lambda b, m, n: (0, n)),
            pl.BlockSpec((D, tn), lambda b, m, n: (0, n + n_steps)),
            pl.BlockSpec((D, tn), lambda b, m, n: (0, n + 2 * n_steps)),
            tab, tab, tab,
        ],
        out_specs=[out, out, out],
        out_shape=[jax.ShapeDtypeStruct((B, N_HEADS, N, VALUE_DIM), BF16), kv_shape, kv_shape],
        scratch_shapes=[pltpu.VMEM((tm, D), BF16)],
        compiler_params=_cparams(("parallel", "parallel", "arbitrary")),
        name="qkv",
    )(x, sh, sc, g, w_qkv, w_qkv, w_qkv, rope_c, rope_slo, rope_shi)


def _kv_ctx_kernel(x_ref, sh_ref, sc_ref, g_ref, wk_ref, wv_ref, k_in, v_in, k_ref, v_ref, h_scr,
                   *, heads_per_step):
    del k_in, v_in

    @pl.when(pl.program_id(1) == 0)
    def _():
        h = _rms(x_ref[0], g_ref[...]) * (1.0 + sc_ref[...]) + sh_ref[...]
        h_scr[...] = h.astype(BF16)

    h = h_scr[...]
    k = _mm(h, wk_ref[...])
    v = _mm(h, wv_ref[...])
    for j in range(heads_per_step):
        sl = slice(j * LANES, (j + 1) * LANES)
        k_ref[0, j] = k[:, sl].astype(BF16)
        v_ref[0, j] = v[:, sl].astype(BF16)


def _kv_ctx_call(ctx, sh, sc, g, w_qkv, k_all, v_all, n_lat):
    B, Lc, _ = ctx.shape
    tn = 512
    hps = tn // LANES
    n_steps = D // tn
    row_blk = n_lat // Lc
    vec = pl.BlockSpec((1, D), lambda b, n: (0, 0))
    out = pl.BlockSpec((1, hps, Lc, LANES), lambda b, n: (b, n, row_blk, 0))
    any_spec = pl.BlockSpec(memory_space=pl.ANY)
    return pl.pallas_call(
        functools.partial(_kv_ctx_kernel, heads_per_step=hps),
        grid=(B, n_steps),
        in_specs=[
            pl.BlockSpec((1, Lc, D), lambda b, n: (b, 0, 0)),
            vec, vec, vec,
            pl.BlockSpec((D, tn), lambda b, n: (0, n + n_steps)),
            pl.BlockSpec((D, tn), lambda b, n: (0, n + 2 * n_steps)),
            any_spec, any_spec,
        ],
        out_specs=[out, out],
        out_shape=[jax.ShapeDtypeStruct(k_all.shape, BF16), jax.ShapeDtypeStruct(v_all.shape, BF16)],
        scratch_shapes=[pltpu.VMEM((Lc, D), BF16)],
        input_output_aliases={6: 0, 7: 1},
        compiler_params=_cparams(("parallel", "arbitrary")),
        name="kv_ctx",
    )(ctx, sh, sc, g, w_qkv, w_qkv, k_all, v_all)


def _attn_kernel(lam_ref, sg_ref, q_ref, k_ref, v_ref, o_ref, *, lambda_init):
    lf = lam_ref[...]
    lam = (jnp.exp(jnp.sum(lf[0:1] * lf[1:2], axis=-1, keepdims=True))
           - jnp.exp(jnp.sum(lf[2:3] * lf[3:4], axis=-1, keepdims=True)) + lambda_init)
    q = q_ref[0, 0]
    k = k_ref[0, 0]
    lane = lax.broadcasted_iota(jnp.int32, q.shape, 1)
    zero = jnp.zeros_like(q)
    dn = (((1,), (1,)), ((), ()))
    s0 = lax.dot_general(jnp.where(lane < HEAD_DIM, q, zero), k, dn, preferred_element_type=F32)
    s1 = lax.dot_general(jnp.where(lane >= HEAD_DIM, q, zero), k, dn, preferred_element_type=F32)
    e0 = jnp.exp(s0 - jnp.max(s0, axis=-1, keepdims=True))
    e1 = jnp.exp(s1 - jnp.max(s1, axis=-1, keepdims=True))
    r0 = 1.0 / jnp.sum(e0, axis=-1, keepdims=True)
    r1 = lam / jnp.sum(e1, axis=-1, keepdims=True)
    a = (e0 * r0 - e1 * r1).astype(BF16)
    o = _mm(a, v_ref[0, 0])
    o = _rms(o, sg_ref[...]) * (1.0 - lambda_init)
    o_ref[0] = o.astype(BF16)


def _attn_call(lam, subln_g, q, k_all, v_all, lambda_init):
    B, H, N, _ = q.shape
    n_keys = k_all.shape[2]
    tq = 128
    kv = pl.BlockSpec((1, 1, n_keys, VALUE_DIM), lambda b, h, i: (b, h, 0, 0))
    return pl.pallas_call(
        functools.partial(_attn_kernel, lambda_init=lambda_init),
        grid=(B, H, N // tq),
        in_specs=[
            pl.BlockSpec((4, HEAD_DIM), lambda b, h, i: (0, 0)),
            pl.BlockSpec((1, VALUE_DIM), lambda b, h, i: (0, 0)),
            pl.BlockSpec((1, 1, tq, VALUE_DIM), lambda b, h, i: (b, h, i, 0)),
            kv, kv,
        ],
        out_specs=pl.BlockSpec((1, tq, VALUE_DIM), lambda b, h, i: (b, i, h)),
        out_shape=jax.ShapeDtypeStruct((B, N, D), BF16),
        compiler_params=_cparams(("parallel", "parallel", "parallel")),
        name="attn",
    )(lam, subln_g.reshape(1, VALUE_DIM), q, k_all, v_all)


def _post_kernel(a_ref, w_ref, x_ref, gt_ref, g_ref, o_ref):
    y = _mm(a_ref[0], w_ref[...])
    o_ref[0] = x_ref[0] + gt_ref[0] * _rms(y, g_ref[...])


def _post_call(a, w, x, gt, g):
    B, N, _ = x.shape
    tm = 512
    row = lambda b, m: (b, m, 0)
    return pl.pallas_call(
        _post_kernel,
        grid=(B, N // tm),
        in_specs=[
            pl.BlockSpec((1, tm, D), row),
            pl.BlockSpec((D, D), lambda b, m: (0, 0)),
            pl.BlockSpec((1, tm, D), row),
            pl.BlockSpec((1, 1, D), lambda b, m: (b, 0, 0)),
            pl.BlockSpec((1, D), lambda b, m: (0, 0)),
        ],
        out_specs=pl.BlockSpec((1, tm, D), row),
        out_shape=jax.ShapeDtypeStruct(x.shape, F32),
        compiler_params=_cparams(("parallel", "parallel")),
        name="post",
    )(a, w, x, gt, g)


def _ffn_kernel(x_ref, sh_ref, sc_ref, gt_ref, g_in_ref, g_out_ref, wg_ref, wu_ref, wd_ref, o_ref,
                h_scr, acc_scr):
    j = pl.program_id(2)

    @pl.when(j == 0)
    def _():
        h = _rms(x_ref[0], g_in_ref[...]) * (1.0 + sc_ref[0]) + sh_ref[0]
        h_scr[...] = h.astype(BF16)
        acc_scr[...] = jnp.zeros_like(acc_scr)

    h = h_scr[...]
    g = _mm(h, wg_ref[...])
    u = _mm(h, wu_ref[...])
    a = (g * jax.nn.sigmoid(g) * u).astype(BF16)
    acc_scr[...] += _mm(a, wd_ref[...])

    @pl.when(j == pl.num_programs(2) - 1)
    def _():
        o_ref[0] = x_ref[0] + gt_ref[0] * _rms(acc_scr[...], g_out_ref[...])


def _ffn_call(x, sh, sc, gt, g_in, g_out, w_gu, w_down):
    B, N, _ = x.shape
    tm, th = 512, 512
    n_chunks = FFN // th
    row = lambda b, m, j: (b, m, 0)
    vec = pl.BlockSpec((1, 1, D), lambda b, m, j: (b, 0, 0))
    gsp = pl.BlockSpec((1, D), lambda b, m, j: (0, 0))
    return pl.pallas_call(
        _ffn_kernel,
        grid=(B, N // tm, n_chunks),
        in_specs=[
            pl.BlockSpec((1, tm, D), row),
            vec, vec, vec, gsp, gsp,
            pl.BlockSpec((D, th), lambda b, m, j: (0, j)),
            pl.BlockSpec((D, th), lambda b, m, j: (0, j + n_chunks)),
            pl.BlockSpec((th, D), lambda b, m, j: (j, 0)),
        ],
        out_specs=pl.BlockSpec((1, tm, D), row),
        out_shape=jax.ShapeDtypeStruct(x.shape, F32),
        scratch_shapes=[pltpu.VMEM((tm, D), BF16), pltpu.VMEM((tm, D), F32)],
        compiler_params=_cparams(("parallel", "parallel", "arbitrary")),
        name="ffn",
    )(x, sh, sc, gt, g_in, g_out, w_gu, w_gu, w_down)


def _fin_kernel(x_ref, sh_ref, sc_ref, g_ref, w_ref, cs_ref, zr_ref, zi_ref):
    h = _rms(x_ref[0], g_ref[...]) * (1.0 + sc_ref[0]) + sh_ref[0]
    u = _mm(h.astype(BF16), w_ref[...]).astype(BF16)
    cs = cs_ref[...]
    for gidx in range(GROUPS):
        sl = slice(gidx * GROUP_DIM, (gidx + 1) * GROUP_DIM)
        z = _mm(u[:, sl], cs)
        zr_ref[0, :, sl] = z[:, :GROUP_DIM].astype(BF16)
        zi_ref[0, :, sl] = z[:, GROUP_DIM:].astype(BF16)


def _fin_call(x, sh, sc, g, w_in, cs):
    B, N, _ = x.shape
    tm = 512
    row = lambda b, m: (b, m, 0)
    vec = pl.BlockSpec((1, 1, D), lambda b, m: (b, 0, 0))
    out = pl.BlockSpec((1, tm, D), row)
    shp = jax.ShapeDtypeStruct((B, N, D), BF16)
    return pl.pallas_call(
        _fin_kernel,
        grid=(B, N // tm),
        in_specs=[
            pl.BlockSpec((1, tm, D), row),
            vec, vec,
            pl.BlockSpec((1, D), lambda b, m: (0, 0)),
            pl.BlockSpec((D, D), lambda b, m: (0, 0)),
            pl.BlockSpec((GROUP_DIM, 2 * GROUP_DIM), lambda b, m: (0, 0)),
        ],
        out_specs=[out, out],
        out_shape=[shp, shp],
        compiler_params=_cparams(("parallel", "parallel")),
        name="fin",
    )(x, sh, sc, g, w_in, cs)


def _dft_a_kernel(zr_ref, zi_ref, m_ref, twc_ref, tws_ref, ar_ref, ai_ref, *, cols_per_step):
    m = m_ref[...]
    for gidx in range(cols_per_step):
        sl = slice(gidx * D, (gidx + 1) * D)
        z = jnp.concatenate([zr_ref[0, :, sl], zi_ref[0, :, sl]], axis=0)
        a = _mm(m, z)
        ar, ai = a[:GRID_W], a[GRID_W:]
        c = jnp.concatenate([twc_ref[gidx]] * (D // LANES), axis=1)
        s = jnp.concatenate([tws_ref[gidx]] * (D // LANES), axis=1)
        ar_ref[0, gidx] = (ar * c + ai * s).astype(BF16)
        ai_ref[0, gidx] = (ai * c - ar * s).astype(BF16)


def _dft_a_call(zr, zi, m_a, twc, tws):
    B = zr.shape[0]
    G = 8
    zin = pl.BlockSpec((1, GRID_W, G * D), lambda b, i: (b, 0, i))
    tw = pl.BlockSpec((G, GRID_W, LANES), lambda b, i: (i, 0, 0))
    out = pl.BlockSpec((1, G, GRID_W, D), lambda b, i: (b, i, 0, 0))
    shp = jax.ShapeDtypeStruct((B, GRID_W, GRID_W, D), BF16)
    return pl.pallas_call(
        functools.partial(_dft_a_kernel, cols_per_step=G),
        grid=(B, GRID_W // G),
        in_specs=[zin, zin, pl.BlockSpec((2 * GRID_W, 2 * GRID_W), lambda b, i: (0, 0)), tw, tw],
        out_specs=[out, out],
        out_shape=[shp, shp],
        compiler_params=_cparams(("parallel", "parallel")),
        name="dft_a",
    )(zr.reshape(B, GRID_W, GRID_W * D), zi.reshape(B, GRID_W, GRID_W * D), m_a, twc, tws)


def _dft_c_kernel(ar_ref, ai_ref, m_ref, f_ref, *, cols_per_step):
    m = m_ref[...]
    for gidx in range(cols_per_step):
        sl = slice(gidx * D, (gidx + 1) * D)
        a = jnp.concatenate([ar_ref[0, :, sl], ai_ref[0, :, sl]], axis=0)
        f_ref[0, :, sl] = _mm(m, a).astype(BF16)


def _dft_c_call(ar, ai, m_c):
    B = ar.shape[0]
    G = 8
    blk = pl.BlockSpec((1, GRID_W, G * D), lambda b, i: (b, 0, i))
    return pl.pallas_call(
        functools.partial(_dft_c_kernel, cols_per_step=G),
        grid=(B, GRID_W // G),
        in_specs=[blk, blk, pl.BlockSpec((GRID_W, 2 * GRID_W), lambda b, i: (0, 0))],
        out_specs=blk,
        out_shape=jax.ShapeDtypeStruct((B, GRID_W, GRID_W * D), BF16),
        compiler_params=_cparams(("parallel", "parallel")),
        name="dft_c",
    )(ar.reshape(B, GRID_W, GRID_W * D), ai.reshape(B, GRID_W, GRID_W * D), m_c)


def _rope_tables(n_tokens):
    pos = jnp.arange(n_tokens)
    lane = jnp.arange(LANES)
    d = lane % HEAD_DIM
    axis = d // (2 * ROPE_FREQS)
    second_half = (d % (2 * ROPE_FREQS)) >= ROPE_FREQS
    inv_freq = ROPE_THETA ** (-(d % ROPE_FREQS).astype(F32) / ROPE_FREQS)
    coord = jnp.where(axis[None, :] == 0, (pos // GRID_W)[:, None], (pos % GRID_W)[:, None])
    ang = coord.astype(F32) * inv_freq[None, :]
    c, s = jnp.cos(ang), jnp.sin(ang)
    return c, jnp.where(second_half[None, :], 0.0, -s), jnp.where(second_half[None, :], s, 0.0)


def _dft_tables():
    two_pi = 2.0 * math.pi
    ch = jnp.arange(GROUP_DIM)
    ang_c = two_pi * ((ch[:, None] * ch[None, :]) % GROUP_DIM).astype(F32) / GROUP_DIM
    ch_scale = GROUP_DIM ** -0.5
    cs = jnp.concatenate([jnp.cos(ang_c), -jnp.sin(ang_c)], axis=1) * ch_scale
    r = jnp.arange(GRID_W)
    ang_r = two_pi * ((r[:, None] * r[None, :]) % GRID_W).astype(F32) / GRID_W
    cr, sr = jnp.cos(ang_r) / 8.0, jnp.sin(ang_r) / 8.0
    m_a = jnp.concatenate([jnp.concatenate([cr, sr], axis=1),
                           jnp.concatenate([-sr, cr], axis=1)], axis=0)
    m_c = jnp.concatenate([cr, sr], axis=1)
    ang_t = two_pi * (r[:, None] * r[None, :]).astype(F32) / (GRID_W * GRID_W)
    twc = jnp.broadcast_to(jnp.cos(ang_t)[:, :, None], (GRID_W, GRID_W, LANES))
    tws = jnp.broadcast_to(jnp.sin(ang_t)[:, :, None], (GRID_W, GRID_W, LANES))
    return cs.astype(BF16), m_a.astype(BF16), m_c.astype(BF16), twc, tws


def kernel(x, c, ctx, c_ctx, mod_w, mod_b, norm_g, ffn_w_gu, ffn_w_down, attn_w_qkv, attn_w_o,
           attn_lambda, attn_subln_g, four_w_in, four_w_out):
    B, N, _ = x.shape
    Lc = ctx.shape[1]
    assert x.shape == (B, GRID_W * GRID_W, D) and ctx.shape == (B, Lc, D) and N % Lc == 0

    cond = jnp.concatenate([c, c_ctx[None, :], jnp.zeros((8 - B - 1, D), F32)], axis=0)
    mods = _ada_call(cond, mod_w, mod_b)

    def lat(layer, idx):
        return mods[layer, :B, idx * D:(idx + 1) * D].reshape(B, 1, D)

    def ctx_mod(layer, idx):
        return mods[layer, B:B + 1, idx * D:(idx + 1) * D]

    g = norm_g.reshape(norm_g.shape[0], 4, 1, D)

    lambda_init = 0.8 - 0.6 * math.exp(-0.3 * 0)
    rope_c, rope_slo, rope_shi = _rope_tables(N)
    w_qkv = attn_w_qkv[0].astype(BF16)
    q, k_all, v_all = _qkv_call(x, lat(0, 0), lat(0, 1), g[0, 0], w_qkv, rope_c, rope_slo, rope_shi, N + Lc)
    k_all, v_all = _kv_ctx_call(ctx, ctx_mod(0, 0), ctx_mod(0, 1), g[0, 0], w_qkv, k_all, v_all, N)
    o = _attn_call(attn_lambda[0], attn_subln_g[0], q, k_all, v_all, lambda_init)
    x = _post_call(o, attn_w_o[0].astype(BF16), x, lat(0, 2), g[0, 1])
    x = _ffn_call(x, lat(0, 3), lat(0, 4), lat(0, 5), g[0, 2], g[0, 3],
                  ffn_w_gu[0].astype(BF16), ffn_w_down[0].astype(BF16))

    cs, m_a, m_c, twc, tws = _dft_tables()
    zr, zi = _fin_call(x, lat(1, 0), lat(1, 1), g[1, 0], four_w_in[0].astype(BF16), cs)
    ar, ai = _dft_a_call(zr, zi, m_a, twc, tws)
    f = _dft_c_call(ar, ai, m_c).reshape(B, N, D)
    x = _post_call(f, four_w_out[0].astype(BF16), x, lat(1, 2), g[1, 1])
    x = _ffn_call(x, lat(1, 3), lat(1, 4), lat(1, 5), g[1, 2], g[1, 3],
                  ffn_w_gu[1].astype(BF16), ffn_w_down[1].astype(BF16))
    return x
```

```python
import functools
import math

import jax
import jax.numpy as jnp
from jax import lax
from jax.experimental import pallas as pl
from jax.experimental.pallas import tpu as pltpu

D = 2048
HEAD_DIM = 64
VALUE_DIM = 2 * HEAD_DIM
N_HEADS = D // VALUE_DIM
GRID_W = 64
ROPE_FREQS = HEAD_DIM // 4
ROPE_THETA = 10000.0
GROUPS = 8
GROUP_DIM = D // GROUPS
FFN = ((8 * D // 3 + 255) // 256) * 256
EPS = 1e-6
LANES = 128
SUBLANES = 8
MXU_TILE = 256
KEY_CHUNK = 2 * MXU_TILE
VT_ROWS = VALUE_DIM + 16
Q_SCALE = HEAD_DIM ** -0.5 * math.log2(math.e)
Q_TILE = MXU_TILE
Q_TILES_PER_STEP = 4
SCORE_LOOKAHEAD = 4
VMEM_LIMIT = 56 * 1024 * 1024

F32 = jnp.float32
BF16 = jnp.bfloat16


def _cparams(sem):
    return pltpu.CompilerParams(dimension_semantics=sem, vmem_limit_bytes=VMEM_LIMIT)


def _rms(x, g):
    return x * lax.rsqrt(jnp.mean(x * x, axis=-1, keepdims=True) + EPS) * g


def _mm(a, b):
    return jnp.dot(a, b, preferred_element_type=F32)


def _ada_kernel(c_ref, w_ref, b_ref, o_ref):
    c = c_ref[...]
    s = (c * jax.nn.sigmoid(c)).astype(BF16)
    o_ref[0] = _mm(s, w_ref[0].astype(BF16)) + b_ref[0]


def _ada_call(cond, mod_w, mod_b):
    depth = mod_w.shape[0]
    tn = 1024
    return pl.pallas_call(
        _ada_kernel,
        grid=(depth, 6 * D // tn),
        in_specs=[
            pl.BlockSpec((8, D), lambda l, n: (0, 0)),
            pl.BlockSpec((1, D, tn), lambda l, n: (l, 0, n)),
            pl.BlockSpec((1, 1, tn), lambda l, n: (l, 0, n)),
        ],
        out_specs=pl.BlockSpec((1, 8, tn), lambda l, n: (l, 0, n)),
        out_shape=jax.ShapeDtypeStruct((depth, 8, 6 * D), F32),
        compiler_params=_cparams(("parallel", "parallel")),
        name="ada",
    )(cond, mod_w, mod_b.reshape(depth, 1, 6 * D))


def _rope(t, c, s_lo, s_hi):
    return t * c + pltpu.roll(t, 16, 1) * s_hi + pltpu.roll(t, LANES - 16, 1) * s_lo


def _store_vt(vt_ref, j, v):
    n = v.shape[0]
    row = lax.broadcasted_iota(jnp.int32, (VT_ROWS - VALUE_DIM, n), 0)
    extra = jnp.where(row == 0, 1.0, 0.0).astype(BF16)
    vt_ref[0, j] = jnp.concatenate([v.T.astype(BF16), extra], axis=0)


def _qkv_kernel(x_ref, sh_ref, sc_ref, g_ref, wq_ref, wk_ref, wv_ref, c_ref, slo_ref, shi_ref,
                q_ref, k_ref, vt_ref, h_scr, *, heads_per_step):
    @pl.when(pl.program_id(2) == 0)
    def _():
        h = _rms(x_ref[0], g_ref[...]) * (1.0 + sc_ref[0]) + sh_ref[0]
        h_scr[...] = h.astype(BF16)

    h = h_scr[...]
    q = _mm(h, wq_ref[...])
    k = _mm(h, wk_ref[...])
    v = _mm(h, wv_ref[...])
    c, s_lo, s_hi = c_ref[...], slo_ref[...], shi_ref[...]
    for j in range(heads_per_step):
        sl = slice(j * LANES, (j + 1) * LANES)
        q_ref[0, j] = (_rope(q[:, sl], c, s_lo, s_hi) * Q_SCALE).astype(BF16)
        k_ref[0, j] = _rope(k[:, sl], c, s_lo, s_hi).astype(BF16)
        _store_vt(vt_ref, j, v[:, sl])


def _qkv_call(x, sh, sc, g, w_qkv, rope_c, rope_slo, rope_shi, n_keys):
    B, N, _ = x.shape
    tm, tn = 512, 512
    hps = tn // LANES
    n_steps = D // tn
    tab = pl.BlockSpec((tm, LANES), lambda b, m, n: (m, 0))
    vec = pl.BlockSpec((1, 1, D), lambda b, m, n: (b, 0, 0))
    out = pl.BlockSpec((1, hps, tm, LANES), lambda b, m, n: (b, n, m, 0))
    return pl.pallas_call(
        functools.partial(_qkv_kernel, heads_per_step=hps),
        grid=(B, N // tm, n_steps),
        in_specs=[
            pl.BlockSpec((1, tm, D), lambda b, m, n: (b, m, 0)),
            vec, vec,
            pl.BlockSpec((1, D), lambda b, m, n: (0, 0)),
            pl.BlockSpec((D, tn), lambda b, m, n: (0, n)),
            pl.BlockSpec((D, tn), lambda b, m, n: (0, n + n_steps)),
            pl.BlockSpec((D, tn), lambda b, m, n: (0, n + 2 * n_steps)),
            tab, tab, tab,
        ],
        out_specs=[out, out,
                   pl.BlockSpec((1, hps, VT_ROWS, tm), lambda b, m, n: (b, n, 0, m))],
        out_shape=[jax.ShapeDtypeStruct((B, N_HEADS, N, VALUE_DIM), BF16),
                   jax.ShapeDtypeStruct((B, N_HEADS, n_keys, VALUE_DIM), BF16),
                   jax.ShapeDtypeStruct((B, N_HEADS, VT_ROWS, n_keys), BF16)],
        scratch_shapes=[pltpu.VMEM((tm, D), BF16)],
        compiler_params=_cparams(("parallel", "parallel", "arbitrary")),
        name="qkv",
    )(x, sh, sc, g, w_qkv, w_qkv, w_qkv, rope_c, rope_slo, rope_shi)


def _kv_ctx_kernel(x_ref, sh_ref, sc_ref, g_ref, wk_ref, wv_ref, k_in, v_in, k_ref, vt_ref, h_scr,
                   *, heads_per_step):
    del k_in, v_in

    @pl.when(pl.program_id(1) == 0)
    def _():
        h = _rms(x_ref[0], g_ref[...]) * (1.0 + sc_ref[...]) + sh_ref[...]
        h_scr[...] = h.astype(BF16)

    h = h_scr[...]
    k = _mm(h, wk_ref[...])
    v = _mm(h, wv_ref[...])
    for j in range(heads_per_step):
        sl = slice(j * LANES, (j + 1) * LANES)
        k_ref[0, j] = k[:, sl].astype(BF16)
        _store_vt(vt_ref, j, v[:, sl])


def _kv_ctx_call(ctx, sh, sc, g, w_qkv, k_all, v_all, n_lat):
    B, Lc, _ = ctx.shape
    assert Lc % LANES == 0 and n_lat % Lc == 0
    tn = 512
    hps = tn // LANES
    n_steps = D // tn
    row_blk = n_lat // Lc
    vec = pl.BlockSpec((1, D), lambda b, n: (0, 0))
    out = pl.BlockSpec((1, hps, Lc, LANES), lambda b, n: (b, n, row_blk, 0))
    out_vt = pl.BlockSpec((1, hps, VT_ROWS, Lc), lambda b, n: (b, n, 0, row_blk))
    any_spec = pl.BlockSpec(memory_space=pl.ANY)
    return pl.pallas_call(
        functools.partial(_kv_ctx_kernel, heads_per_step=hps),
        grid=(B, n_steps),
        in_specs=[
            pl.BlockSpec((1, Lc, D), lambda b, n: (b, 0, 0)),
            vec, vec, vec,
            pl.BlockSpec((D, tn), lambda b, n: (0, n + n_steps)),
            pl.BlockSpec((D, tn), lambda b, n: (0, n + 2 * n_steps)),
            any_spec, any_spec,
        ],
        out_specs=[out, out_vt],
        out_shape=[jax.ShapeDtypeStruct(k_all.shape, BF16), jax.ShapeDtypeStruct(v_all.shape, BF16)],
        scratch_shapes=[pltpu.VMEM((Lc, D), BF16)],
        input_output_aliases={6: 0, 7: 1},
        compiler_params=_cparams(("parallel", "arbitrary")),
        name="kv_ctx",
    )(ctx, sh, sc, g, w_qkv, w_qkv, k_all, v_all)


def _sublane_fold(x, op):
    return op(x.reshape(x.shape[0] // SUBLANES, SUBLANES, x.shape[1]), axis=0)


def _attn_kernel(lam_ref, sg_ref, q_ref, k_ref, vt_ref, o_ref, pv_scr, *, lambda_init, chunks):
    lf = lam_ref[...]
    lam = (jnp.exp(jnp.sum(lf[0:1] * lf[1:2], axis=-1, keepdims=True))
           - jnp.exp(jnp.sum(lf[2:3] * lf[3:4], axis=-1, keepdims=True)) + lambda_init)
    n_tiles = q_ref.shape[2] // Q_TILE
    lane = lax.broadcasted_iota(jnp.int32, (Q_TILE, VALUE_DIM), 1)
    zero = jnp.zeros((Q_TILE, VALUE_DIM), BF16)
    q_maps = []
    for ti in range(n_tiles):
        q = q_ref[0, 0, ti * Q_TILE:(ti + 1) * Q_TILE, :]
        q_maps.append((jnp.where(lane < HEAD_DIM, q, zero), jnp.where(lane >= HEAD_DIM, q, zero)))
    dn = (((1,), (1,)), ((), ()))
    tasks = [(ti, ci, mp) for ti in range(n_tiles) for ci in range(len(chunks)) for mp in range(2)]

    def scores(t):
        ti, ci, mp = tasks[t]
        kc = k_ref[0, 0, chunks[ci][0]:chunks[ci][1], :]
        return lax.dot_general(kc, q_maps[ti][mp], dn, preferred_element_type=F32)

    def merge(ti, maxes):
        ot = None
        for mp in range(2):
            m = functools.reduce(jnp.maximum, maxes[mp])
            w = [jnp.exp2(mc - m) for mc in maxes[mp]]
            den = functools.reduce(
                jnp.add, [wc * pv_scr[ti, mp, ci, VALUE_DIM:VALUE_DIM + 1, :] for ci, wc in enumerate(w)])
            norm = 1.0 / den if mp == 0 else -lam / den
            for ci, wc in enumerate(w):
                term = pv_scr[ti, mp, ci, :VALUE_DIM, :] * (wc * norm)
                ot = term if ot is None else ot + term
        ot = ot * lax.rsqrt(jnp.mean(ot * ot, axis=0, keepdims=True) + EPS)
        o_ref[0, ti * Q_TILE:(ti + 1) * Q_TILE, :] = (
            ot.T * (sg_ref[...] * (1.0 - lambda_init))).astype(BF16)

    pending = [scores(t) for t in range(SCORE_LOOKAHEAD)]
    maxes = ([], [])
    for t, (ti, ci, mp) in enumerate(tasks):
        if t + SCORE_LOOKAHEAD < len(tasks):
            pending.append(scores(t + SCORE_LOOKAHEAD))
        s = pending.pop(0)
        mc = jnp.max(_sublane_fold(s, jnp.max), axis=0, keepdims=True)
        e = jnp.exp2(s - mc).astype(BF16)
        maxes[mp].append(mc)
        pv_scr[ti, mp, ci] = _mm(vt_ref[0, 0, :, chunks[ci][0]:chunks[ci][1]], e)
        if (ci, mp) == (len(chunks) - 1, 1):
            merge(ti, maxes)
            maxes = ([], [])


def _key_chunks(n_keys):
    return tuple((lo, min(lo + KEY_CHUNK, n_keys)) for lo in range(0, n_keys, KEY_CHUNK))


def _attn_call(lam, subln_g, q, k_all, vt_all, lambda_init):
    B, H, N, _ = q.shape
    n_keys = k_all.shape[2]
    assert n_keys % MXU_TILE == 0
    chunks = _key_chunks(n_keys)
    tq = Q_TILE * Q_TILES_PER_STEP
    return pl.pallas_call(
        functools.partial(_attn_kernel, lambda_init=lambda_init, chunks=chunks),
        grid=(B, H, N // tq),
        in_specs=[
            pl.BlockSpec((4, HEAD_DIM), lambda b, h, i: (0, 0)),
            pl.BlockSpec((1, VALUE_DIM), lambda b, h, i: (0, 0)),
            pl.BlockSpec((1, 1, tq, VALUE_DIM), lambda b, h, i: (b, h, i, 0)),
            pl.BlockSpec((1, 1, n_keys, VALUE_DIM), lambda b, h, i: (b, h, 0, 0)),
            pl.BlockSpec((1, 1, VT_ROWS, n_keys), lambda b, h, i: (b, h, 0, 0)),
        ],
        out_specs=pl.BlockSpec((1, tq, VALUE_DIM), lambda b, h, i: (b, i, h)),
        out_shape=jax.ShapeDtypeStruct((B, N, D), BF16),
        scratch_shapes=[pltpu.VMEM((Q_TILES_PER_STEP, 2, len(chunks), VT_ROWS, Q_TILE), F32)],
        compiler_params=_cparams(("parallel", "parallel", "parallel")),
        name="attn",
    )(lam, subln_g.reshape(1, VALUE_DIM), q, k_all, vt_all)


def _post_kernel(a_ref, w_ref, x_ref, gt_ref, g_ref, o_ref, *, pitched):
    if pitched:
        a = jnp.concatenate([a_ref[0, _slab(r), :] for r in range(a_ref.shape[1] // PITCH)], axis=0)
    else:
        a = a_ref[0]
    y = _mm(a.astype(BF16), w_ref[...])
    o_ref[0] = x_ref[0] + gt_ref[0] * _rms(y, g_ref[...])


def _post_call(a, w, x, gt, g, pitched=False):
    B, N, _ = x.shape
    tm = 512
    row = lambda b, m: (b, m, 0)
    return pl.pallas_call(
        functools.partial(_post_kernel, pitched=pitched),
        grid=(B, N // tm),
        in_specs=[
            pl.BlockSpec((1, tm // GRID_W * PITCH if pitched else tm, D), row),
            pl.BlockSpec((D, D), lambda b, m: (0, 0)),
            pl.BlockSpec((1, tm, D), row),
            pl.BlockSpec((1, 1, D), lambda b, m: (b, 0, 0)),
            pl.BlockSpec((1, D), lambda b, m: (0, 0)),
        ],
        out_specs=pl.BlockSpec((1, tm, D), row),
        out_shape=jax.ShapeDtypeStruct(x.shape, F32),
        compiler_params=_cparams(("parallel", "parallel")),
        name="post",
    )(a, w, x, gt, g)


def _ffn_kernel(x_ref, sh_ref, sc_ref, gt_ref, g_in_ref, g_out_ref, wg_ref, wu_ref, wd_ref, o_ref,
                h_scr, acc_scr):
    j = pl.program_id(2)

    @pl.when(j == 0)
    def _():
        h = _rms(x_ref[0], g_in_ref[...]) * (1.0 + sc_ref[0]) + sh_ref[0]
        h_scr[...] = h.astype(BF16)
        acc_scr[...] = jnp.zeros_like(acc_scr)

    h = h_scr[...]
    g = _mm(h, wg_ref[...])
    u = _mm(h, wu_ref[...])
    a = (g * jax.nn.sigmoid(g) * u).astype(BF16)
    acc_scr[...] += _mm(a, wd_ref[...])

    @pl.when(j == pl.num_programs(2) - 1)
    def _():
        o_ref[0] = x_ref[0] + gt_ref[0] * _rms(acc_scr[...], g_out_ref[...])


def _ffn_call(x, sh, sc, gt, g_in, g_out, w_gu, w_down):
    B, N, _ = x.shape
    tm, th = 512, 512
    n_chunks = FFN // th
    row = lambda b, m, j: (b, m, 0)
    vec = pl.BlockSpec((1, 1, D), lambda b, m, j: (b, 0, 0))
    gsp = pl.BlockSpec((1, D), lambda b, m, j: (0, 0))
    return pl.pallas_call(
        _ffn_kernel,
        grid=(B, N // tm, n_chunks),
        in_specs=[
            pl.BlockSpec((1, tm, D), row),
            vec, vec, vec, gsp, gsp,
            pl.BlockSpec((D, th), lambda b, m, j: (0, j)),
            pl.BlockSpec((D, th), lambda b, m, j: (0, j + n_chunks)),
            pl.BlockSpec((th, D), lambda b, m, j: (j, 0)),
        ],
        out_specs=pl.BlockSpec((1, tm, D), row),
        out_shape=jax.ShapeDtypeStruct(x.shape, F32),
        scratch_shapes=[pltpu.VMEM((tm, D), BF16), pltpu.VMEM((tm, D), F32)],
        compiler_params=_cparams(("parallel", "parallel", "arbitrary")),
        name="ffn",
    )(x, sh, sc, gt, g_in, g_out, w_gu, w_gu, w_down)


PITCH = GRID_W + SUBLANES
N_PITCHED = GRID_W * PITCH


def _slab(i):
    return slice(i * PITCH, i * PITCH + GRID_W)


def _pad(i):
    return slice(i * PITCH + GRID_W, (i + 1) * PITCH)


def _complex_rows(re_ref, im_ref, col):
    rows = pl.ds(col, GRID_W, stride=PITCH)
    return jnp.concatenate([re_ref[0, rows, :].astype(BF16), im_ref[0, rows, :].astype(BF16)], axis=0)


def _fin_kernel(x_ref, sh_ref, sc_ref, g_ref, w_ref, cs_ref, zr_ref, zi_ref):
    h = _rms(x_ref[0], g_ref[...]) * (1.0 + sc_ref[0]) + sh_ref[0]
    u = _mm(h.astype(BF16), w_ref[...]).astype(BF16)
    cs = cs_ref[...]
    rows_per_step = x_ref.shape[1] // GRID_W
    for gidx in range(GROUPS):
        sl = slice(gidx * GROUP_DIM, (gidx + 1) * GROUP_DIM)
        z = _mm(u[:, sl], cs)
        for r in range(rows_per_step):
            zr_ref[0, _slab(r), sl] = z[r * GRID_W:(r + 1) * GRID_W, :GROUP_DIM]
            zi_ref[0, _slab(r), sl] = z[r * GRID_W:(r + 1) * GRID_W, GROUP_DIM:]
    for r in range(rows_per_step):
        zr_ref[0, _pad(r), :] = jnp.zeros((PITCH - GRID_W, D), F32)
        zi_ref[0, _pad(r), :] = jnp.zeros((PITCH - GRID_W, D), F32)


def _fin_call(x, sh, sc, g, w_in, cs):
    B, N, _ = x.shape
    tm = 512
    row = lambda b, m: (b, m, 0)
    vec = pl.BlockSpec((1, 1, D), lambda b, m: (b, 0, 0))
    out = pl.BlockSpec((1, tm // GRID_W * PITCH, D), row)
    shp = jax.ShapeDtypeStruct((B, N_PITCHED, D), F32)
    return pl.pallas_call(
        _fin_kernel,
        grid=(B, N // tm),
        in_specs=[
            pl.BlockSpec((1, tm, D), row),
            vec, vec,
            pl.BlockSpec((1, D), lambda b, m: (0, 0)),
            pl.BlockSpec((D, D), lambda b, m: (0, 0)),
            pl.BlockSpec((GROUP_DIM, 2 * GROUP_DIM), lambda b, m: (0, 0)),
        ],
        out_specs=[out, out],
        out_shape=[shp, shp],
        compiler_params=_cparams(("parallel", "parallel")),
        name="fin",
    )(x, sh, sc, g, w_in, cs)


def _dft_a_kernel(zr_ref, zi_ref, m_ref, twc_ref, tws_ref, ar_ref, ai_ref):
    m = m_ref[...]
    zero_pad = jnp.zeros((PITCH - GRID_W, zr_ref.shape[2]), F32)
    for col in range(GRID_W):
        a = _mm(m, _complex_rows(zr_ref, zi_ref, col))
        ar, ai = a[:GRID_W], a[GRID_W:]
        c, s = twc_ref[col], tws_ref[col]
        ar_ref[0, _slab(col), :] = ar * c + ai * s
        ai_ref[0, _slab(col), :] = ai * c - ar * s
        ar_ref[0, _pad(col), :] = zero_pad
        ai_ref[0, _pad(col), :] = zero_pad


def _dft_a_call(zr, zi, m_a, twc, tws):
    B = zr.shape[0]
    blk = pl.BlockSpec((1, N_PITCHED, LANES), lambda b, i: (b, 0, i))
    tw = pl.BlockSpec((GRID_W, GRID_W, LANES), lambda b, i: (0, 0, 0))
    shp = jax.ShapeDtypeStruct((B, N_PITCHED, D), F32)
    return pl.pallas_call(
        _dft_a_kernel,
        grid=(B, D // LANES),
        in_specs=[blk, blk, pl.BlockSpec((2 * GRID_W, 2 * GRID_W), lambda b, i: (0, 0)), tw, tw],
        out_specs=[blk, blk],
        out_shape=[shp, shp],
        compiler_params=_cparams(("parallel", "parallel")),
        name="dft_a",
    )(zr, zi, m_a, twc, tws)


def _dft_c_kernel(ar_ref, ai_ref, m_ref, f_ref):
    m = m_ref[...]
    for kr in range(GRID_W):
        f = _mm(m, _complex_rows(ar_ref, ai_ref, kr))
        f_ref[0, pl.ds(kr, GRID_W, stride=PITCH), :] = f
    for j in range(GRID_W, PITCH):
        f_ref[0, pl.ds(j, GRID_W, stride=PITCH), :] = jnp.zeros((GRID_W, f_ref.shape[2]), F32)


def _dft_c_call(ar, ai, m_c):
    B = ar.shape[0]
    blk = pl.BlockSpec((1, N_PITCHED, LANES), lambda b, i: (b, 0, i))
    return pl.pallas_call(
        _dft_c_kernel,
        grid=(B, D // LANES),
        in_specs=[blk, blk, pl.BlockSpec((GRID_W, 2 * GRID_W), lambda b, i: (0, 0))],
        out_specs=blk,
        out_shape=jax.ShapeDtypeStruct((B, N_PITCHED, D), F32),
        compiler_params=_cparams(("parallel", "parallel")),
        name="dft_c",
    )(ar, ai, m_c)


def _rope_tables(n_tokens):
    pos = jnp.arange(n_tokens)
    lane = jnp.arange(LANES)
    d = lane % HEAD_DIM
    axis = d // (2 * ROPE_FREQS)
    second_half = (d % (2 * ROPE_FREQS)) >= ROPE_FREQS
    inv_freq = ROPE_THETA ** (-(d % ROPE_FREQS).astype(F32) / ROPE_FREQS)
    coord = jnp.where(axis[None, :] == 0, (pos // GRID_W)[:, None], (pos % GRID_W)[:, None])
    ang = coord.astype(F32) * inv_freq[None, :]
    c, s = jnp.cos(ang), jnp.sin(ang)
    return c, jnp.where(second_half[None, :], 0.0, -s), jnp.where(second_half[None, :], s, 0.0)


def _dft_tables():
    two_pi = 2.0 * math.pi
    ch = jnp.arange(GROUP_DIM)
    ang_c = two_pi * ((ch[:, None] * ch[None, :]) % GROUP_DIM).astype(F32) / GROUP_DIM
    ch_scale = GROUP_DIM ** -0.5
    cs = jnp.concatenate([jnp.cos(ang_c), -jnp.sin(ang_c)], axis=1) * ch_scale
    r = jnp.arange(GRID_W)
    ang_r = two_pi * ((r[:, None] * r[None, :]) % GRID_W).astype(F32) / GRID_W
    cr, sr = jnp.cos(ang_r) / 8.0, jnp.sin(ang_r) / 8.0
    m_a = jnp.concatenate([jnp.concatenate([cr, sr], axis=1),
                           jnp.concatenate([-sr, cr], axis=1)], axis=0)
    m_c = jnp.concatenate([cr, sr], axis=1)
    ang_t = two_pi * (r[:, None] * r[None, :]).astype(F32) / (GRID_W * GRID_W)
    twc = jnp.broadcast_to(jnp.cos(ang_t)[:, :, None], (GRID_W, GRID_W, LANES))
    tws = jnp.broadcast_to(jnp.sin(ang_t)[:, :, None], (GRID_W, GRID_W, LANES))
    return cs.astype(BF16), m_a.astype(BF16), m_c.astype(BF16), twc, tws


def kernel(x, c, ctx, c_ctx, mod_w, mod_b, norm_g, ffn_w_gu, ffn_w_down, attn_w_qkv, attn_w_o,
           attn_lambda, attn_subln_g, four_w_in, four_w_out):
    B, N, _ = x.shape
    Lc = ctx.shape[1]
    assert x.shape == (B, GRID_W * GRID_W, D) and ctx.shape == (B, Lc, D) and N % Lc == 0

    cond = jnp.concatenate([c, c_ctx[None, :], jnp.zeros((8 - B - 1, D), F32)], axis=0)
    mods = _ada_call(cond, mod_w, mod_b)

    def lat(layer, idx):
        return mods[layer, :B, idx * D:(idx + 1) * D].reshape(B, 1, D)

    def ctx_mod(layer, idx):
        return mods[layer, B:B + 1, idx * D:(idx + 1) * D]

    g = norm_g.reshape(norm_g.shape[0], 4, 1, D)

    lambda_init = 0.8 - 0.6 * math.exp(-0.3 * 0)
    rope_c, rope_slo, rope_shi = _rope_tables(N)
    w_qkv = attn_w_qkv[0].astype(BF16)
    q, k_all, v_all = _qkv_call(x, lat(0, 0), lat(0, 1), g[0, 0], w_qkv, rope_c, rope_slo, rope_shi, N + Lc)
    k_all, v_all = _kv_ctx_call(ctx, ctx_mod(0, 0), ctx_mod(0, 1), g[0, 0], w_qkv, k_all, v_all, N)
    o = _attn_call(attn_lambda[0], attn_subln_g[0], q, k_all, v_all, lambda_init)
    x = _post_call(o, attn_w_o[0].astype(BF16), x, lat(0, 2), g[0, 1])
    x = _ffn_call(x, lat(0, 3), lat(0, 4), lat(0, 5), g[0, 2], g[0, 3],
                  ffn_w_gu[0].astype(BF16), ffn_w_down[0].astype(BF16))

    cs, m_a, m_c, twc, tws = _dft_tables()
    zr, zi = _fin_call(x, lat(1, 0), lat(1, 1), g[1, 0], four_w_in[0].astype(BF16), cs)
    f = _dft_c_call(*_dft_a_call(zr, zi, m_a, twc, tws), m_c)
    x = _post_call(f, four_w_out[0].astype(BF16), x, lat(1, 2), g[1, 1], pitched=True)
    x = _ffn_call(x, lat(1, 3), lat(1, 4), lat(1, 5), g[1, 2], g[1, 3],
                  ffn_w_gu[1].astype(BF16), ffn_w_down[1].astype(BF16))
    return x
```

```python
import functools
import math

import jax
import jax.numpy as jnp
from jax import lax
from jax.experimental import pallas as pl
from jax.experimental.pallas import tpu as pltpu

D = 2048
HEAD_DIM = 64
VALUE_DIM = 2 * HEAD_DIM
N_HEADS = D // VALUE_DIM
GRID_W = 64
ROPE_FREQS = HEAD_DIM // 4
ROPE_THETA = 10000.0
GROUPS = 8
GROUP_DIM = D // GROUPS
FFN = ((8 * D // 3 + 255) // 256) * 256
EPS = 1e-6
LANES = 128
SUBLANES = 8
MXU_TILE = 256
KEY_CHUNK = 2 * MXU_TILE
VT_ROWS = VALUE_DIM + 16
Q_SCALE = HEAD_DIM ** -0.5 * math.log2(math.e)
Q_TILE = MXU_TILE
Q_TILES_PER_STEP = 4
SCORE_LOOKAHEAD = 4
VMEM_LIMIT = 56 * 1024 * 1024

F32 = jnp.float32
BF16 = jnp.bfloat16


def _cparams(sem):
    return pltpu.CompilerParams(dimension_semantics=sem, vmem_limit_bytes=VMEM_LIMIT)


def _rms(x, g):
    return x * lax.rsqrt(jnp.mean(x * x, axis=-1, keepdims=True) + EPS) * g


def _mm(a, b):
    return jnp.dot(a, b, preferred_element_type=F32)


def _modulated_norm(x, g, sc, sh):
    return (_rms(x, g) * (1.0 + sc) + sh).astype(BF16)


def _lookahead_rows(step, n_steps, tm):
    rows = -(-tm // n_steps)
    rows += -rows % 16
    start = jnp.minimum(step * rows, tm - rows)
    return pl.ds(pl.multiple_of(start, 16), rows)


def _next_tile(n_tiles):
    return lambda t: jnp.minimum(t + 1, n_tiles - 1)


def _ada_kernel(c_ref, w_ref, b_ref, o_ref):
    c = c_ref[...]
    s = (c * jax.nn.sigmoid(c)).astype(BF16)
    o_ref[0] = _mm(s, w_ref[0].astype(BF16)) + b_ref[0]


def _ada_call(cond, mod_w, mod_b):
    depth = mod_w.shape[0]
    tn = 1024
    return pl.pallas_call(
        _ada_kernel,
        grid=(depth, 6 * D // tn),
        in_specs=[
            pl.BlockSpec((8, D), lambda l, n: (0, 0)),
            pl.BlockSpec((1, D, tn), lambda l, n: (l, 0, n)),
            pl.BlockSpec((1, 1, tn), lambda l, n: (l, 0, n)),
        ],
        out_specs=pl.BlockSpec((1, 8, tn), lambda l, n: (l, 0, n)),
        out_shape=jax.ShapeDtypeStruct((depth, 8, 6 * D), F32),
        compiler_params=_cparams(("parallel", "parallel")),
        name="ada",
    )(cond, mod_w, mod_b.reshape(depth, 1, 6 * D))


def _rope(t, c, s_lo, s_hi):
    return t * c + pltpu.roll(t, 16, 1) * s_hi + pltpu.roll(t, LANES - 16, 1) * s_lo


def _store_vt(vt_ref, j, v):
    n = v.shape[0]
    row = lax.broadcasted_iota(jnp.int32, (VT_ROWS - VALUE_DIM, n), 0)
    extra = jnp.where(row == 0, 1.0, 0.0).astype(BF16)
    vt_ref[0, j] = jnp.concatenate([v.T.astype(BF16), extra], axis=0)


def _qkv_kernel(x_ref, xn_ref, sh_ref, sc_ref, shn_ref, scn_ref, g_ref, wq_ref, wk_ref, wv_ref,
                c_ref, slo_ref, shi_ref, q_ref, k_ref, vt_ref, h_scr, *, heads_per_step, n_steps):
    t, n = pl.program_id(0), pl.program_id(1)
    cur = lax.rem(t, 2)

    @pl.when((t == 0) & (n == 0))
    def _():
        h_scr[0] = _modulated_norm(x_ref[0], g_ref[...], sc_ref[0], sh_ref[0])

    h = h_scr[cur]
    q = _mm(h, wq_ref[...])
    k = _mm(h, wk_ref[...])
    v = _mm(h, wv_ref[...])
    c, s_lo, s_hi = c_ref[...], slo_ref[...], shi_ref[...]
    for j in range(heads_per_step):
        sl = slice(j * LANES, (j + 1) * LANES)
        q_ref[0, j] = (_rope(q[:, sl], c, s_lo, s_hi) * Q_SCALE).astype(BF16)
        k_ref[0, j] = _rope(k[:, sl], c, s_lo, s_hi).astype(BF16)
        _store_vt(vt_ref, j, v[:, sl])

    rows = _lookahead_rows(n, n_steps, x_ref.shape[1])
    h_scr[1 - cur, rows, :] = _modulated_norm(xn_ref[0, rows, :], g_ref[...], scn_ref[0], shn_ref[0])


def _qkv_call(x, sh, sc, g, w_qkv, rope_c, rope_slo, rope_shi):
    B, N, _ = x.shape
    tm, tn = 512, 512
    hps = tn // LANES
    n_steps = D // tn
    tiles = N // tm
    nxt = _next_tile(B * tiles)
    row = pl.BlockSpec((1, tm, D), lambda t, n: (t // tiles, t % tiles, 0))
    row_next = pl.BlockSpec((1, tm, D), lambda t, n: (nxt(t) // tiles, nxt(t) % tiles, 0))
    vec = pl.BlockSpec((1, 1, D), lambda t, n: (t // tiles, 0, 0))
    vec_next = pl.BlockSpec((1, 1, D), lambda t, n: (nxt(t) // tiles, 0, 0))
    tab = pl.BlockSpec((tm, LANES), lambda t, n: (t % tiles, 0))
    out = pl.BlockSpec((1, hps, tm, LANES), lambda t, n: (t // tiles, n, t % tiles, 0))
    return pl.pallas_call(
        functools.partial(_qkv_kernel, heads_per_step=hps, n_steps=n_steps),
        grid=(B * tiles, n_steps),
        in_specs=[
            row, row_next, vec, vec, vec_next, vec_next,
            pl.BlockSpec((1, D), lambda t, n: (0, 0)),
            pl.BlockSpec((D, tn), lambda t, n: (0, n)),
            pl.BlockSpec((D, tn), lambda t, n: (0, n + n_steps)),
            pl.BlockSpec((D, tn), lambda t, n: (0, n + 2 * n_steps)),
            tab, tab, tab,
        ],
        out_specs=[out, out,
                   pl.BlockSpec((1, hps, VT_ROWS, tm), lambda t, n: (t // tiles, n, 0, t % tiles))],
        out_shape=[jax.ShapeDtypeStruct((B, N_HEADS, N, VALUE_DIM), BF16),
                   jax.ShapeDtypeStruct((B, N_HEADS, N, VALUE_DIM), BF16),
                   jax.ShapeDtypeStruct((B, N_HEADS, VT_ROWS, N), BF16)],
        scratch_shapes=[pltpu.VMEM((2, tm, D), BF16)],
        compiler_params=_cparams(("arbitrary", "arbitrary")),
        name="qkv",
    )(x, x, sh, sc, sh, sc, g, w_qkv, w_qkv, w_qkv, rope_c, rope_slo, rope_shi)


def _kv_ctx_kernel(x_ref, sh_ref, sc_ref, g_ref, wk_ref, wv_ref, k_ref, vt_ref, h_scr, *, heads_per_step):
    @pl.when(pl.program_id(1) == 0)
    def _():
        h_scr[...] = _modulated_norm(x_ref[0], g_ref[...], sc_ref[...], sh_ref[...])

    h = h_scr[...]
    k = _mm(h, wk_ref[...])
    v = _mm(h, wv_ref[...])
    for j in range(heads_per_step):
        sl = slice(j * LANES, (j + 1) * LANES)
        k_ref[0, j] = k[:, sl].astype(BF16)
        _store_vt(vt_ref, j, v[:, sl])


def _kv_ctx_call(ctx, sh, sc, g, w_qkv):
    B, Lc, _ = ctx.shape
    assert Lc % LANES == 0
    tn = 512
    hps = tn // LANES
    n_steps = D // tn
    vec = pl.BlockSpec((1, D), lambda b, n: (0, 0))
    return pl.pallas_call(
        functools.partial(_kv_ctx_kernel, heads_per_step=hps),
        grid=(B, n_steps),
        in_specs=[
            pl.BlockSpec((1, Lc, D), lambda b, n: (b, 0, 0)),
            vec, vec, vec,
            pl.BlockSpec((D, tn), lambda b, n: (0, n + n_steps)),
            pl.BlockSpec((D, tn), lambda b, n: (0, n + 2 * n_steps)),
        ],
        out_specs=[pl.BlockSpec((1, hps, Lc, LANES), lambda b, n: (b, n, 0, 0)),
                   pl.BlockSpec((1, hps, VT_ROWS, Lc), lambda b, n: (b, n, 0, 0))],
        out_shape=[jax.ShapeDtypeStruct((B, N_HEADS, Lc, VALUE_DIM), BF16),
                   jax.ShapeDtypeStruct((B, N_HEADS, VT_ROWS, Lc), BF16)],
        scratch_shapes=[pltpu.VMEM((Lc, D), BF16)],
        compiler_params=_cparams(("parallel", "arbitrary")),
        name="kv_ctx",
    )(ctx, sh, sc, g, w_qkv, w_qkv)


def _sublane_fold(x, op):
    return op(x.reshape(x.shape[0] // SUBLANES, SUBLANES, x.shape[1]), axis=0)


def _attn_kernel(lam_ref, sg_ref, q_ref, kl_ref, vtl_ref, kc_ref, vtc_ref, o_ref, pv_scr,
                 *, lambda_init, chunks):
    lf = lam_ref[...]
    lam = (jnp.exp(jnp.sum(lf[0:1] * lf[1:2], axis=-1, keepdims=True))
           - jnp.exp(jnp.sum(lf[2:3] * lf[3:4], axis=-1, keepdims=True)) + lambda_init)
    n_tiles = q_ref.shape[2] // Q_TILE
    lane = lax.broadcasted_iota(jnp.int32, (Q_TILE, VALUE_DIM), 1)
    zero = jnp.zeros((Q_TILE, VALUE_DIM), BF16)
    q_maps = []
    for ti in range(n_tiles):
        q = q_ref[0, 0, ti * Q_TILE:(ti + 1) * Q_TILE, :]
        q_maps.append((jnp.where(lane < HEAD_DIM, q, zero), jnp.where(lane >= HEAD_DIM, q, zero)))
    dn = (((1,), (1,)), ((), ()))
    tasks = [(ti, ci, mp) for ti in range(n_tiles) for ci in range(len(chunks)) for mp in range(2)]

    k_refs, vt_refs = (kl_ref, kc_ref), (vtl_ref, vtc_ref)

    def scores(t):
        ti, ci, mp = tasks[t]
        src, lo, hi = chunks[ci]
        return lax.dot_general(k_refs[src][0, 0, lo:hi, :], q_maps[ti][mp], dn,
                               preferred_element_type=F32)

    def merge(ti, maxes):
        ot = None
        for mp in range(2):
            m = functools.reduce(jnp.maximum, maxes[mp])
            w = [jnp.exp2(mc - m) for mc in maxes[mp]]
            den = functools.reduce(
                jnp.add, [wc * pv_scr[ti, mp, ci, VALUE_DIM:VALUE_DIM + 1, :] for ci, wc in enumerate(w)])
            norm = 1.0 / den if mp == 0 else -lam / den
            for ci, wc in enumerate(w):
                term = pv_scr[ti, mp, ci, :VALUE_DIM, :] * (wc * norm)
                ot = term if ot is None else ot + term
        ot = ot * lax.rsqrt(jnp.mean(ot * ot, axis=0, keepdims=True) + EPS)
        o_ref[0, ti * Q_TILE:(ti + 1) * Q_TILE, :] = (
            ot.T * (sg_ref[...] * (1.0 - lambda_init))).astype(BF16)

    pending = [scores(t) for t in range(SCORE_LOOKAHEAD)]
    maxes = ([], [])
    for t, (ti, ci, mp) in enumerate(tasks):
        if t + SCORE_LOOKAHEAD < len(tasks):
            pending.append(scores(t + SCORE_LOOKAHEAD))
        s = pending.pop(0)
        mc = jnp.max(_sublane_fold(s, jnp.max), axis=0, keepdims=True)
        e = jnp.exp2(s - mc).astype(BF16)
        maxes[mp].append(mc)
        src, lo, hi = chunks[ci]
        pv_scr[ti, mp, ci] = _mm(vt_refs[src][0, 0, :, lo:hi], e)
        if (ci, mp) == (len(chunks) - 1, 1):
            merge(ti, maxes)
            maxes = ([], [])


def _key_chunks(src, n_keys):
    assert n_keys % MXU_TILE == 0
    return tuple((src, lo, min(lo + KEY_CHUNK, n_keys)) for lo in range(0, n_keys, KEY_CHUNK))


def _attn_call(lam, subln_g, q, k_lat, vt_lat, k_ctx, vt_ctx, lambda_init):
    B, H, N, _ = q.shape
    n_ctx = k_ctx.shape[2]
    chunks = _key_chunks(0, N) + _key_chunks(1, n_ctx)
    tq = Q_TILE * Q_TILES_PER_STEP
    per_head = lambda b, h, i: (b, h, 0, 0)
    return pl.pallas_call(
        functools.partial(_attn_kernel, lambda_init=lambda_init, chunks=chunks),
        grid=(B, H, N // tq),
        in_specs=[
            pl.BlockSpec((4, HEAD_DIM), lambda b, h, i: (0, 0)),
            pl.BlockSpec((1, VALUE_DIM), lambda b, h, i: (0, 0)),
            pl.BlockSpec((1, 1, tq, VALUE_DIM), lambda b, h, i: (b, h, i, 0)),
            pl.BlockSpec((1, 1, N, VALUE_DIM), per_head),
            pl.BlockSpec((1, 1, VT_ROWS, N), per_head),
            pl.BlockSpec((1, 1, n_ctx, VALUE_DIM), per_head),
            pl.BlockSpec((1, 1, VT_ROWS, n_ctx), per_head),
        ],
        out_specs=pl.BlockSpec((1, tq, VALUE_DIM), lambda b, h, i: (b, i, h)),
        out_shape=jax.ShapeDtypeStruct((B, N, D), BF16),
        scratch_shapes=[pltpu.VMEM((Q_TILES_PER_STEP, 2, len(chunks), VT_ROWS, Q_TILE), F32)],
        compiler_params=_cparams(("parallel", "parallel", "parallel")),
        name="attn",
    )(lam, subln_g.reshape(1, VALUE_DIM), q, k_lat, vt_lat, k_ctx, vt_ctx)


def _post_kernel(a_ref, w_ref, x_ref, gt_ref, g_ref, o_ref, *, pitched):
    if pitched:
        a = jnp.concatenate([a_ref[0, _slab(r), :] for r in range(a_ref.shape[1] // PITCH)], axis=0)
    else:
        a = a_ref[0]
    y = _mm(a.astype(BF16), w_ref[...])
    o_ref[0] = x_ref[0] + gt_ref[0] * _rms(y, g_ref[...])


def _post_call(a, w, x, gt, g, pitched=False):
    B, N, _ = x.shape
    tm = 512
    row = lambda b, m: (b, m, 0)
    return pl.pallas_call(
        functools.partial(_post_kernel, pitched=pitched),
        grid=(B, N // tm),
        in_specs=[
            pl.BlockSpec((1, tm // GRID_W * PITCH if pitched else tm, D), row),
            pl.BlockSpec((D, D), lambda b, m: (0, 0)),
            pl.BlockSpec((1, tm, D), row),
            pl.BlockSpec((1, 1, D), lambda b, m: (b, 0, 0)),
            pl.BlockSpec((1, D), lambda b, m: (0, 0)),
        ],
        out_specs=pl.BlockSpec((1, tm, D), row),
        out_shape=jax.ShapeDtypeStruct(x.shape, F32),
        compiler_params=_cparams(("parallel", "parallel")),
        name="post",
    )(a, w, x, gt, g)


def _ffn_kernel(x_ref, xn_ref, sh_ref, sc_ref, shn_ref, scn_ref, gt_ref, g_in_ref, g_out_ref,
                wg_ref, wu_ref, wd_ref, o_ref, h_scr, acc_scr, *, n_chunks):
    t, j = pl.program_id(0), pl.program_id(1)
    cur = lax.rem(t, 2)

    @pl.when((t == 0) & (j == 0))
    def _():
        h_scr[0] = _modulated_norm(x_ref[0], g_in_ref[...], sc_ref[0], sh_ref[0])

    @pl.when(j == 0)
    def _():
        acc_scr[...] = jnp.zeros_like(acc_scr)

    h = h_scr[cur]
    g = _mm(h, wg_ref[...])
    u = _mm(h, wu_ref[...])
    a = (g * jax.nn.sigmoid(g) * u).astype(BF16)
    acc_scr[...] += _mm(a, wd_ref[...])

    rows = _lookahead_rows(j, n_chunks, x_ref.shape[1])
    h_scr[1 - cur, rows, :] = _modulated_norm(xn_ref[0, rows, :], g_in_ref[...], scn_ref[0], shn_ref[0])

    @pl.when(j == n_chunks - 1)
    def _():
        o_ref[0] = x_ref[0] + gt_ref[0] * _rms(acc_scr[...], g_out_ref[...])


def _ffn_call(x, sh, sc, gt, g_in, g_out, w_gu, w_down, layer):
    B, N, _ = x.shape
    tm, th = 512, 512
    n_chunks = FFN // th
    tiles = N // tm
    n_tiles = B * tiles

    def nxt(t):
        return jnp.minimum(t + 1, n_tiles - 1)

    row = pl.BlockSpec((1, tm, D), lambda t, j: (t // tiles, t % tiles, 0))
    row_next = pl.BlockSpec((1, tm, D), lambda t, j: (nxt(t) // tiles, nxt(t) % tiles, 0))
    vec = pl.BlockSpec((1, 1, D), lambda t, j: (t // tiles, 0, 0))
    vec_next = pl.BlockSpec((1, 1, D), lambda t, j: (nxt(t) // tiles, 0, 0))
    gsp = pl.BlockSpec((1, D), lambda t, j: (0, 0))
    return pl.pallas_call(
        functools.partial(_ffn_kernel, n_chunks=n_chunks),
        grid=(n_tiles, n_chunks),
        in_specs=[
            row, row_next, vec, vec, vec_next, vec_next, vec, gsp, gsp,
            pl.BlockSpec((None, D, th), lambda t, j: (layer, 0, j)),
            pl.BlockSpec((None, D, th), lambda t, j: (layer, 0, j + n_chunks)),
            pl.BlockSpec((None, th, D), lambda t, j: (layer, j, 0)),
        ],
        out_specs=row,
        out_shape=jax.ShapeDtypeStruct(x.shape, F32),
        scratch_shapes=[pltpu.VMEM((2, tm, D), BF16), pltpu.VMEM((tm, D), F32)],
        compiler_params=_cparams(("arbitrary", "arbitrary")),
        name="ffn",
    )(x, x, sh, sc, sh, sc, gt, g_in, g_out, w_gu, w_gu, w_down)


PITCH = GRID_W + SUBLANES
N_PITCHED = GRID_W * PITCH


def _slab(i):
    return slice(i * PITCH, i * PITCH + GRID_W)


def _pad(i):
    return slice(i * PITCH + GRID_W, (i + 1) * PITCH)


def _complex_rows(re_ref, im_ref, col):
    rows = pl.ds(col, GRID_W, stride=PITCH)
    return jnp.concatenate([re_ref[0, rows, :].astype(BF16), im_ref[0, rows, :].astype(BF16)], axis=0)


def _fin_kernel(x_ref, xn_ref, sh_ref, sc_ref, shn_ref, scn_ref, g_ref, w_ref, u_ref, h_scr):
    t = pl.program_id(0)
    cur = lax.rem(t, 2)

    @pl.when(t == 0)
    def _():
        h_scr[0] = _modulated_norm(x_ref[0], g_ref[...], sc_ref[0], sh_ref[0])

    u = _mm(h_scr[cur], w_ref[...])
    for r in range(x_ref.shape[1] // GRID_W):
        u_ref[0, _slab(r), :] = u[r * GRID_W:(r + 1) * GRID_W]
        u_ref[0, _pad(r), :] = jnp.zeros((PITCH - GRID_W, D), F32)
    h_scr[1 - cur] = _modulated_norm(xn_ref[0], g_ref[...], scn_ref[0], shn_ref[0])


def _fin_call(x, sh, sc, g, w_in):
    B, N, _ = x.shape
    tm = 512
    tiles = N // tm
    nxt = _next_tile(B * tiles)
    row = pl.BlockSpec((1, tm, D), lambda t: (t // tiles, t % tiles, 0))
    row_next = pl.BlockSpec((1, tm, D), lambda t: (nxt(t) // tiles, nxt(t) % tiles, 0))
    vec = pl.BlockSpec((1, 1, D), lambda t: (t // tiles, 0, 0))
    vec_next = pl.BlockSpec((1, 1, D), lambda t: (nxt(t) // tiles, 0, 0))
    return pl.pallas_call(
        _fin_kernel,
        grid=(B * tiles,),
        in_specs=[
            row, row_next, vec, vec, vec_next, vec_next,
            pl.BlockSpec((1, D), lambda t: (0, 0)),
            pl.BlockSpec((D, D), lambda t: (0, 0)),
        ],
        out_specs=pl.BlockSpec((1, tm // GRID_W * PITCH, D), lambda t: (t // tiles, t % tiles, 0)),
        out_shape=jax.ShapeDtypeStruct((B, N_PITCHED, D), F32),
        scratch_shapes=[pltpu.VMEM((2, tm, D), BF16)],
        compiler_params=_cparams(("arbitrary",)),
        name="fin",
    )(x, x, sh, sc, sh, sc, g, w_in)


def _dft_a_kernel(ulo_ref, uhi_ref, cs_ref, m_ref, twc_ref, tws_ref, ar_ref, ai_ref):
    cs, m = cs_ref[...], m_ref[...]
    zero_pad = jnp.zeros((PITCH - GRID_W, GROUP_DIM), F32)
    batch = MXU_TILE // GRID_W
    for col0 in range(0, GRID_W, batch):
        u = []
        for col in range(col0, col0 + batch):
            rows = pl.ds(col, GRID_W, stride=PITCH)
            u.append(jnp.concatenate([ulo_ref[0, rows, :], uhi_ref[0, rows, :]], axis=1).astype(BF16))
        zb = _mm(jnp.concatenate(u, axis=0), cs)
        for i, col in enumerate(range(col0, col0 + batch)):
            z = zb[i * GRID_W:(i + 1) * GRID_W]
            z = jnp.concatenate([z[:, :GROUP_DIM], z[:, GROUP_DIM:]], axis=0).astype(BF16)
            a = _mm(m, z)
            ar, ai = a[:GRID_W], a[GRID_W:]
            c = jnp.concatenate([twc_ref[col]] * (GROUP_DIM // LANES), axis=1)
            s = jnp.concatenate([tws_ref[col]] * (GROUP_DIM // LANES), axis=1)
            ar_ref[0, _slab(col), :] = ar * c + ai * s
            ai_ref[0, _slab(col), :] = ai * c - ar * s
            ar_ref[0, _pad(col), :] = zero_pad
            ai_ref[0, _pad(col), :] = zero_pad


def _dft_a_call(u, cs, m_a, twc, tws):
    B = u.shape[0]
    halves = GROUP_DIM // LANES
    assert halves == 2
    lane_blk = lambda h: pl.BlockSpec((1, N_PITCHED, LANES), lambda b, i: (b, 0, halves * i + h))
    out = pl.BlockSpec((1, N_PITCHED, GROUP_DIM), lambda b, i: (b, 0, i))
    tw = pl.BlockSpec((GRID_W, GRID_W, LANES), lambda b, i: (0, 0, 0))
    shp = jax.ShapeDtypeStruct((B, N_PITCHED, D), F32)
    return pl.pallas_call(
        _dft_a_kernel,
        grid=(B, GROUPS),
        in_specs=[
            lane_blk(0), lane_blk(1),
            pl.BlockSpec((GROUP_DIM, 2 * GROUP_DIM), lambda b, i: (0, 0)),
            pl.BlockSpec((2 * GRID_W, 2 * GRID_W), lambda b, i: (0, 0)),
            tw, tw,
        ],
        out_specs=[out, out],
        out_shape=[shp, shp],
        compiler_params=_cparams(("parallel", "parallel")),
        name="dft_a",
    )(u, u, cs, m_a, twc, tws)


def _dft_c_kernel(ar_ref, ai_ref, m_ref, f_ref):
    m = m_ref[...]
    for kr in range(GRID_W):
        f = _mm(m, _complex_rows(ar_ref, ai_ref, kr))
        f_ref[0, pl.ds(kr, GRID_W, stride=PITCH), :] = f
    for j in range(GRID_W, PITCH):
        f_ref[0, pl.ds(j, GRID_W, stride=PITCH), :] = jnp.zeros((GRID_W, f_ref.shape[2]), F32)


def _dft_c_call(ar, ai, m_c):
    B = ar.shape[0]
    blk = pl.BlockSpec((1, N_PITCHED, LANES), lambda b, i: (b, 0, i))
    return pl.pallas_call(
        _dft_c_kernel,
        grid=(B, D // LANES),
        in_specs=[blk, blk, pl.BlockSpec((GRID_W, 2 * GRID_W), lambda b, i: (0, 0))],
        out_specs=blk,
        out_shape=jax.ShapeDtypeStruct((B, N_PITCHED, D), F32),
        compiler_params=_cparams(("parallel", "parallel")),
        name="dft_c",
    )(ar, ai, m_c)


def _rope_tables(n_tokens):
    pos = jnp.arange(n_tokens)
    lane = jnp.arange(LANES)
    d = lane % HEAD_DIM
    axis = d // (2 * ROPE_FREQS)
    second_half = (d % (2 * ROPE_FREQS)) >= ROPE_FREQS
    inv_freq = ROPE_THETA ** (-(d % ROPE_FREQS).astype(F32) / ROPE_FREQS)
    coord = jnp.where(axis[None, :] == 0, (pos // GRID_W)[:, None], (pos % GRID_W)[:, None])
    ang = coord.astype(F32) * inv_freq[None, :]
    c, s = jnp.cos(ang), jnp.sin(ang)
    return c, jnp.where(second_half[None, :], 0.0, -s), jnp.where(second_half[None, :], s, 0.0)


def _dft_tables():
    two_pi = 2.0 * math.pi
    ch = jnp.arange(GROUP_DIM)
    ang_c = two_pi * ((ch[:, None] * ch[None, :]) % GROUP_DIM).astype(F32) / GROUP_DIM
    ch_scale = GROUP_DIM ** -0.5
    cs = jnp.concatenate([jnp.cos(ang_c), -jnp.sin(ang_c)], axis=1) * ch_scale
    r = jnp.arange(GRID_W)
    ang_r = two_pi * ((r[:, None] * r[None, :]) % GRID_W).astype(F32) / GRID_W
    cr, sr = jnp.cos(ang_r) / 8.0, jnp.sin(ang_r) / 8.0
    m_a = jnp.concatenate([jnp.concatenate([cr, sr], axis=1),
                           jnp.concatenate([-sr, cr], axis=1)], axis=0)
    m_c = jnp.concatenate([cr, sr], axis=1)
    ang_t = two_pi * (r[:, None] * r[None, :]).astype(F32) / (GRID_W * GRID_W)
    twc = jnp.broadcast_to(jnp.cos(ang_t)[:, :, None], (GRID_W, GRID_W, LANES))
    tws = jnp.broadcast_to(jnp.sin(ang_t)[:, :, None], (GRID_W, GRID_W, LANES))
    return cs.astype(BF16), m_a.astype(BF16), m_c.astype(BF16), twc, tws


def kernel(x, c, ctx, c_ctx, mod_w, mod_b, norm_g, ffn_w_gu, ffn_w_down, attn_w_qkv, attn_w_o,
           attn_lambda, attn_subln_g, four_w_in, four_w_out):
    B, N, _ = x.shape
    Lc = ctx.shape[1]
    assert x.shape == (B, GRID_W * GRID_W, D) and ctx.shape == (B, Lc, D) and N % Lc == 0

    cond = jnp.concatenate([c, c_ctx[None, :], jnp.zeros((8 - B - 1, D), F32)], axis=0)
    mods = _ada_call(cond, mod_w, mod_b)

    def lat(layer, idx):
        return mods[layer, :B, idx * D:(idx + 1) * D].reshape(B, 1, D)

    def ctx_mod(layer, idx):
        return mods[layer, B:B + 1, idx * D:(idx + 1) * D]

    g = norm_g.reshape(norm_g.shape[0], 4, 1, D)

    lambda_init = 0.8 - 0.6 * math.exp(-0.3 * 0)
    rope_c, rope_slo, rope_shi = _rope_tables(N)
    w_qkv = attn_w_qkv[0].astype(BF16)
    q, k_lat, vt_lat = _qkv_call(x, lat(0, 0), lat(0, 1), g[0, 0], w_qkv, rope_c, rope_slo, rope_shi)
    k_ctx, vt_ctx = _kv_ctx_call(ctx, ctx_mod(0, 0), ctx_mod(0, 1), g[0, 0], w_qkv)
    o = _attn_call(attn_lambda[0], attn_subln_g[0], q, k_lat, vt_lat, k_ctx, vt_ctx, lambda_init)
    x = _post_call(o, attn_w_o[0].astype(BF16), x, lat(0, 2), g[0, 1])
    w_gu, w_down = ffn_w_gu.astype(BF16), ffn_w_down.astype(BF16)
    x = _ffn_call(x, lat(0, 3), lat(0, 4), lat(0, 5), g[0, 2], g[0, 3], w_gu, w_down, 0)

    cs, m_a, m_c, twc, tws = _dft_tables()
    u = _fin_call(x, lat(1, 0), lat(1, 1), g[1, 0], four_w_in[0].astype(BF16))
    f = _dft_c_call(*_dft_a_call(u, cs, m_a, twc, tws), m_c)
    x = _post_call(f, four_w_out[0].astype(BF16), x, lat(1, 2), g[1, 1], pitched=True)
    x = _ffn_call(x, lat(1, 3), lat(1, 4), lat(1, 5), g[1, 2], g[1, 3], w_gu, w_down, 1)
    return x
```

```python
import functools
import math

import jax
import jax.numpy as jnp
from jax import lax
from jax.experimental import pallas as pl
from jax.experimental.pallas import tpu as pltpu

D = 2048
HEAD_DIM = 64
VALUE_DIM = 2 * HEAD_DIM
N_HEADS = D // VALUE_DIM
GRID_W = 64
ROPE_FREQS = HEAD_DIM // 4
ROPE_THETA = 10000.0
GROUPS = 8
GROUP_DIM = D // GROUPS
FFN = ((8 * D // 3 + 255) // 256) * 256
EPS = 1e-6
LANES = 128
SUBLANES = 8
MXU_TILE = 256
KEY_CHUNK = 2 * MXU_TILE
VT_ROWS = VALUE_DIM + 16
Q_SCALE = HEAD_DIM ** -0.5 * math.log2(math.e)
Q_TILE = MXU_TILE
Q_TILES_PER_STEP = 4
SCORE_LOOKAHEAD = 4
VMEM_LIMIT = 56 * 1024 * 1024

F32 = jnp.float32
BF16 = jnp.bfloat16


def _cparams(sem):
    return pltpu.CompilerParams(dimension_semantics=sem, vmem_limit_bytes=VMEM_LIMIT)


def _rms(x, g):
    return x * lax.rsqrt(jnp.mean(x * x, axis=-1, keepdims=True) + EPS) * g


def _mm(a, b):
    return jnp.dot(a, b, preferred_element_type=F32)


def _modulated_norm(x, g, sc, sh):
    return (_rms(x, g) * (1.0 + sc) + sh).astype(BF16)


def _lookahead_rows(step, n_steps, tm):
    rows = -(-tm // n_steps)
    rows += -rows % 16
    start = jnp.minimum(step * rows, tm - rows)
    return pl.ds(pl.multiple_of(start, 16), rows)


def _next_tile(n_tiles):
    return lambda t: jnp.minimum(t + 1, n_tiles - 1)


def _ada_kernel(c_ref, w_ref, b_ref, o_ref):
    c = c_ref[...]
    s = (c * jax.nn.sigmoid(c)).astype(BF16)
    o_ref[0] = _mm(s, w_ref[0].astype(BF16)) + b_ref[0]


def _ada_call(cond, mod_w, mod_b):
    depth = mod_w.shape[0]
    tn = 1024
    return pl.pallas_call(
        _ada_kernel,
        grid=(depth, 6 * D // tn),
        in_specs=[
            pl.BlockSpec((8, D), lambda l, n: (0, 0)),
            pl.BlockSpec((1, D, tn), lambda l, n: (l, 0, n)),
            pl.BlockSpec((1, 1, tn), lambda l, n: (l, 0, n)),
        ],
        out_specs=pl.BlockSpec((1, 8, tn), lambda l, n: (l, 0, n)),
        out_shape=jax.ShapeDtypeStruct((depth, 8, 6 * D), F32),
        compiler_params=_cparams(("parallel", "parallel")),
        name="ada",
    )(cond, mod_w, mod_b.reshape(depth, 1, 6 * D))


def _rope(t, c, s_lo, s_hi):
    return t * c + pltpu.roll(t, 16, 1) * s_hi + pltpu.roll(t, LANES - 16, 1) * s_lo


def _store_vt(vt_ref, j, v):
    n = v.shape[0]
    row = lax.broadcasted_iota(jnp.int32, (VT_ROWS - VALUE_DIM, n), 0)
    extra = jnp.where(row == 0, 1.0, 0.0).astype(BF16)
    vt_ref[0, j] = jnp.concatenate([v.T.astype(BF16), extra], axis=0)


def _qkv_kernel(x_ref, xn_ref, sh_ref, sc_ref, shn_ref, scn_ref, g_ref, wq_ref, wk_ref, wv_ref,
                c_ref, slo_ref, shi_ref, q_ref, k_ref, vt_ref, h_scr, *, heads_per_step, n_steps):
    t, n = pl.program_id(0), pl.program_id(1)
    cur = lax.rem(t, 2)

    @pl.when((t == 0) & (n == 0))
    def _():
        h_scr[0] = _modulated_norm(x_ref[0], g_ref[...], sc_ref[0], sh_ref[0])

    h = h_scr[cur]
    q = _mm(h, wq_ref[...])
    k = _mm(h, wk_ref[...])
    v = _mm(h, wv_ref[...])
    c, s_lo, s_hi = c_ref[...], slo_ref[...], shi_ref[...]
    for j in range(heads_per_step):
        sl = slice(j * LANES, (j + 1) * LANES)
        q_ref[0, j] = (_rope(q[:, sl], c, s_lo, s_hi) * Q_SCALE).astype(BF16)
        k_ref[0, j] = _rope(k[:, sl], c, s_lo, s_hi).astype(BF16)
        _store_vt(vt_ref, j, v[:, sl])

    rows = _lookahead_rows(n, n_steps, x_ref.shape[1])
    h_scr[1 - cur, rows, :] = _modulated_norm(xn_ref[0, rows, :], g_ref[...], scn_ref[0], shn_ref[0])


def _qkv_call(x, sh, sc, g, w_qkv, rope_c, rope_slo, rope_shi):
    B, N, _ = x.shape
    tm, tn = 512, 512
    hps = tn // LANES
    n_steps = D // tn
    tiles = N // tm
    nxt = _next_tile(B * tiles)
    row = pl.BlockSpec((1, tm, D), lambda t, n: (t // tiles, t % tiles, 0))
    row_next = pl.BlockSpec((1, tm, D), lambda t, n: (nxt(t) // tiles, nxt(t) % tiles, 0))
    vec = pl.BlockSpec((1, 1, D), lambda t, n: (t // tiles, 0, 0))
    vec_next = pl.BlockSpec((1, 1, D), lambda t, n: (nxt(t) // tiles, 0, 0))
    tab = pl.BlockSpec((tm, LANES), lambda t, n: (t % tiles, 0))
    out = pl.BlockSpec((1, hps, tm, LANES), lambda t, n: (t // tiles, n, t % tiles, 0))
    return pl.pallas_call(
        functools.partial(_qkv_kernel, heads_per_step=hps, n_steps=n_steps),
        grid=(B * tiles, n_steps),
        in_specs=[
            row, row_next, vec, vec, vec_next, vec_next,
            pl.BlockSpec((1, D), lambda t, n: (0, 0)),
            pl.BlockSpec((D, tn), lambda t, n: (0, n)),
            pl.BlockSpec((D, tn), lambda t, n: (0, n + n_steps)),
            pl.BlockSpec((D, tn), lambda t, n: (0, n + 2 * n_steps)),
            tab, tab, tab,
        ],
        out_specs=[out, out,
                   pl.BlockSpec((1, hps, VT_ROWS, tm), lambda t, n: (t // tiles, n, 0, t % tiles))],
        out_shape=[jax.ShapeDtypeStruct((B, N_HEADS, N, VALUE_DIM), BF16),
                   jax.ShapeDtypeStruct((B, N_HEADS, N, VALUE_DIM), BF16),
                   jax.ShapeDtypeStruct((B, N_HEADS, VT_ROWS, N), BF16)],
        scratch_shapes=[pltpu.VMEM((2, tm, D), BF16)],
        compiler_params=_cparams(("arbitrary", "arbitrary")),
        name="qkv",
    )(x, x, sh, sc, sh, sc, g, w_qkv, w_qkv, w_qkv, rope_c, rope_slo, rope_shi)


def _kv_ctx_kernel(x_ref, sh_ref, sc_ref, g_ref, wk_ref, wv_ref, k_ref, vt_ref, h_scr, *, heads_per_step):
    @pl.when(pl.program_id(1) == 0)
    def _():
        h_scr[...] = _modulated_norm(x_ref[0], g_ref[...], sc_ref[...], sh_ref[...])

    h = h_scr[...]
    k = _mm(h, wk_ref[...])
    v = _mm(h, wv_ref[...])
    for j in range(heads_per_step):
        sl = slice(j * LANES, (j + 1) * LANES)
        k_ref[0, j] = k[:, sl].astype(BF16)
        _store_vt(vt_ref, j, v[:, sl])


def _kv_ctx_call(ctx, sh, sc, g, w_qkv):
    B, Lc, _ = ctx.shape
    assert Lc % LANES == 0
    tn = 512
    hps = tn // LANES
    n_steps = D // tn
    vec = pl.BlockSpec((1, D), lambda b, n: (0, 0))
    return pl.pallas_call(
        functools.partial(_kv_ctx_kernel, heads_per_step=hps),
        grid=(B, n_steps),
        in_specs=[
            pl.BlockSpec((1, Lc, D), lambda b, n: (b, 0, 0)),
            vec, vec, vec,
            pl.BlockSpec((D, tn), lambda b, n: (0, n + n_steps)),
            pl.BlockSpec((D, tn), lambda b, n: (0, n + 2 * n_steps)),
        ],
        out_specs=[pl.BlockSpec((1, hps, Lc, LANES), lambda b, n: (b, n, 0, 0)),
                   pl.BlockSpec((1, hps, VT_ROWS, Lc), lambda b, n: (b, n, 0, 0))],
        out_shape=[jax.ShapeDtypeStruct((B, N_HEADS, Lc, VALUE_DIM), BF16),
                   jax.ShapeDtypeStruct((B, N_HEADS, VT_ROWS, Lc), BF16)],
        scratch_shapes=[pltpu.VMEM((Lc, D), BF16)],
        compiler_params=_cparams(("parallel", "arbitrary")),
        name="kv_ctx",
    )(ctx, sh, sc, g, w_qkv, w_qkv)


def _sublane_fold(x, op):
    return op(x.reshape(x.shape[0] // SUBLANES, SUBLANES, x.shape[1]), axis=0)


def _attn_kernel(lam_ref, sg_ref, q_ref, kl_ref, vtl_ref, kc_ref, vtc_ref, *rest, lambda_init, chunks):
    n_cast = (len(rest) - 2) // 2
    o_ref, pv_scr = rest[n_cast], rest[-1]
    for w_ref, w_out in zip(rest[:n_cast], rest[n_cast + 1:-1]):
        w_out[...] = w_ref[...].astype(BF16)
    lf = lam_ref[...]
    lam = (jnp.exp(jnp.sum(lf[0:1] * lf[1:2], axis=-1, keepdims=True))
           - jnp.exp(jnp.sum(lf[2:3] * lf[3:4], axis=-1, keepdims=True)) + lambda_init)
    n_tiles = q_ref.shape[2] // Q_TILE
    lane = lax.broadcasted_iota(jnp.int32, (Q_TILE, VALUE_DIM), 1)
    zero = jnp.zeros((Q_TILE, VALUE_DIM), BF16)
    q_maps = []
    for ti in range(n_tiles):
        q = q_ref[0, 0, ti * Q_TILE:(ti + 1) * Q_TILE, :]
        q_maps.append((jnp.where(lane < HEAD_DIM, q, zero), jnp.where(lane >= HEAD_DIM, q, zero)))
    dn = (((1,), (1,)), ((), ()))
    tasks = [(ti, ci, mp) for ti in range(n_tiles) for ci in range(len(chunks)) for mp in range(2)]

    k_refs, vt_refs = (kl_ref, kc_ref), (vtl_ref, vtc_ref)

    def scores(t):
        ti, ci, mp = tasks[t]
        src, lo, hi = chunks[ci]
        return lax.dot_general(k_refs[src][0, 0, lo:hi, :], q_maps[ti][mp], dn,
                               preferred_element_type=F32)

    def merge(ti, maxes):
        ot = None
        for mp in range(2):
            m = functools.reduce(jnp.maximum, maxes[mp])
            w = [jnp.exp2(mc - m) for mc in maxes[mp]]
            den = functools.reduce(
                jnp.add, [wc * pv_scr[ti, mp, ci, VALUE_DIM:VALUE_DIM + 1, :] for ci, wc in enumerate(w)])
            norm = 1.0 / den if mp == 0 else -lam / den
            for ci, wc in enumerate(w):
                term = pv_scr[ti, mp, ci, :VALUE_DIM, :] * (wc * norm)
                ot = term if ot is None else ot + term
        ot = ot * lax.rsqrt(jnp.mean(ot * ot, axis=0, keepdims=True) + EPS)
        o_ref[0, ti * Q_TILE:(ti + 1) * Q_TILE, :] = (
            ot.T * (sg_ref[...] * (1.0 - lambda_init))).astype(BF16)

    pending = [scores(t) for t in range(SCORE_LOOKAHEAD)]
    maxes = ([], [])
    for t, (ti, ci, mp) in enumerate(tasks):
        if t + SCORE_LOOKAHEAD < len(tasks):
            pending.append(scores(t + SCORE_LOOKAHEAD))
        s = pending.pop(0)
        mc = jnp.max(_sublane_fold(s, jnp.max), axis=0, keepdims=True)
        e = jnp.exp2(s - mc).astype(BF16)
        maxes[mp].append(mc)
        src, lo, hi = chunks[ci]
        pv_scr[ti, mp, ci] = _mm(vt_refs[src][0, 0, :, lo:hi], e)
        if (ci, mp) == (len(chunks) - 1, 1):
            merge(ti, maxes)
            maxes = ([], [])


def _key_chunks(src, n_keys):
    assert n_keys % MXU_TILE == 0
    return tuple((src, lo, min(lo + KEY_CHUNK, n_keys)) for lo in range(0, n_keys, KEY_CHUNK))


def _cast_rows(n_rows, n_steps):
    rows = next(r for r in range(16, n_rows + 1, 16) if n_rows % r == 0 and r * n_steps >= n_rows)
    return rows, n_rows // rows


def _attn_call(lam, subln_g, q, k_lat, vt_lat, k_ctx, vt_ctx, lambda_init, weights):
    B, H, N, _ = q.shape
    n_ctx = k_ctx.shape[2]
    chunks = _key_chunks(0, N) + _key_chunks(1, n_ctx)
    tq = Q_TILE * Q_TILES_PER_STEP
    n_q = N // tq
    per_head = lambda b, h, i: (b, h, 0, 0)

    def cast_spec(w):
        rows, n_blocks = _cast_rows(w.shape[0], B * H * n_q)
        return pl.BlockSpec((rows, w.shape[1]),
                            lambda b, h, i: (jnp.minimum((b * H + h) * n_q + i, n_blocks - 1), 0))

    return pl.pallas_call(
        functools.partial(_attn_kernel, lambda_init=lambda_init, chunks=chunks),
        grid=(B, H, n_q),
        in_specs=[
            pl.BlockSpec((4, HEAD_DIM), lambda b, h, i: (0, 0)),
            pl.BlockSpec((1, VALUE_DIM), lambda b, h, i: (0, 0)),
            pl.BlockSpec((1, 1, tq, VALUE_DIM), lambda b, h, i: (b, h, i, 0)),
            pl.BlockSpec((1, 1, N, VALUE_DIM), per_head),
            pl.BlockSpec((1, 1, VT_ROWS, N), per_head),
            pl.BlockSpec((1, 1, n_ctx, VALUE_DIM), per_head),
            pl.BlockSpec((1, 1, VT_ROWS, n_ctx), per_head),
            *[cast_spec(w) for w in weights],
        ],
        out_specs=[pl.BlockSpec((1, tq, VALUE_DIM), lambda b, h, i: (b, i, h)),
                   *[cast_spec(w) for w in weights]],
        out_shape=[jax.ShapeDtypeStruct((B, N, D), BF16),
                   *[jax.ShapeDtypeStruct(w.shape, BF16) for w in weights]],
        scratch_shapes=[pltpu.VMEM((Q_TILES_PER_STEP, 2, len(chunks), VT_ROWS, Q_TILE), F32)],
        compiler_params=_cparams(("arbitrary", "arbitrary", "arbitrary")),
        name="attn",
    )(lam, subln_g.reshape(1, VALUE_DIM), q, k_lat, vt_lat, k_ctx, vt_ctx, *weights)


def _post_kernel(a_ref, w_ref, x_ref, gt_ref, g_ref, o_ref, *, pitched):
    if pitched:
        a = jnp.concatenate([a_ref[0, _slab(r), :] for r in range(a_ref.shape[1] // PITCH)], axis=0)
    else:
        a = a_ref[0]
    y = _mm(a.astype(BF16), w_ref[...])
    o_ref[0] = x_ref[0] + gt_ref[0] * _rms(y, g_ref[...])


def _post_call(a, w, x, gt, g, pitched=False):
    B, N, _ = x.shape
    tm = 512
    row = lambda b, m: (b, m, 0)
    return pl.pallas_call(
        functools.partial(_post_kernel, pitched=pitched),
        grid=(B, N // tm),
        in_specs=[
            pl.BlockSpec((1, tm // GRID_W * PITCH if pitched else tm, D), row),
            pl.BlockSpec((D, D), lambda b, m: (0, 0)),
            pl.BlockSpec((1, tm, D), row),
            pl.BlockSpec((1, 1, D), lambda b, m: (b, 0, 0)),
            pl.BlockSpec((1, D), lambda b, m: (0, 0)),
        ],
        out_specs=pl.BlockSpec((1, tm, D), row),
        out_shape=jax.ShapeDtypeStruct(x.shape, F32),
        compiler_params=_cparams(("parallel", "parallel")),
        name="post",
    )(a, w, x, gt, g)


def _ffn_kernel(x_ref, xn_ref, sh_ref, sc_ref, shn_ref, scn_ref, gt_ref, g_in_ref, g_out_ref,
                wg_ref, wu_ref, wd_ref, o_ref, h_scr, acc_scr, *, n_chunks):
    t, j = pl.program_id(0), pl.program_id(1)
    cur = lax.rem(t, 2)

    @pl.when((t == 0) & (j == 0))
    def _():
        h_scr[0] = _modulated_norm(x_ref[0], g_in_ref[...], sc_ref[0], sh_ref[0])

    @pl.when(j == 0)
    def _():
        acc_scr[...] = jnp.zeros_like(acc_scr)

    h = h_scr[cur]
    g = _mm(h, wg_ref[...])
    u = _mm(h, wu_ref[...])
    a = (g * jax.nn.sigmoid(g) * u).astype(BF16)
    acc_scr[...] += _mm(a, wd_ref[...])

    rows = _lookahead_rows(j, n_chunks, x_ref.shape[1])
    h_scr[1 - cur, rows, :] = _modulated_norm(xn_ref[0, rows, :], g_in_ref[...], scn_ref[0], shn_ref[0])

    @pl.when(j == n_chunks - 1)
    def _():
        o_ref[0] = x_ref[0] + gt_ref[0] * _rms(acc_scr[...], g_out_ref[...])


def _ffn_call(x, sh, sc, gt, g_in, g_out, w_gu, w_down, layer):
    B, N, _ = x.shape
    tm, th = 512, 512
    n_chunks = FFN // th
    tiles = N // tm
    n_tiles = B * tiles

    def nxt(t):
        return jnp.minimum(t + 1, n_tiles - 1)

    row = pl.BlockSpec((1, tm, D), lambda t, j: (t // tiles, t % tiles, 0))
    row_next = pl.BlockSpec((1, tm, D), lambda t, j: (nxt(t) // tiles, nxt(t) % tiles, 0))
    vec = pl.BlockSpec((1, 1, D), lambda t, j: (t // tiles, 0, 0))
    vec_next = pl.BlockSpec((1, 1, D), lambda t, j: (nxt(t) // tiles, 0, 0))
    gsp = pl.BlockSpec((1, D), lambda t, j: (0, 0))
    return pl.pallas_call(
        functools.partial(_ffn_kernel, n_chunks=n_chunks),
        grid=(n_tiles, n_chunks),
        in_specs=[
            row, row_next, vec, vec, vec_next, vec_next, vec, gsp, gsp,
            pl.BlockSpec((None, D, th), lambda t, j: (layer, 0, j)),
            pl.BlockSpec((None, D, th), lambda t, j: (layer, 0, j + n_chunks)),
            pl.BlockSpec((None, th, D), lambda t, j: (layer, j, 0)),
        ],
        out_specs=row,
        out_shape=jax.ShapeDtypeStruct(x.shape, F32),
        scratch_shapes=[pltpu.VMEM((2, tm, D), BF16), pltpu.VMEM((tm, D), F32)],
        compiler_params=_cparams(("arbitrary", "arbitrary")),
        name="ffn",
    )(x, x, sh, sc, sh, sc, gt, g_in, g_out, w_gu, w_gu, w_down)


PITCH = GRID_W + SUBLANES
N_PITCHED = GRID_W * PITCH


def _slab(i):
    return slice(i * PITCH, i * PITCH + GRID_W)


def _pad(i):
    return slice(i * PITCH + GRID_W, (i + 1) * PITCH)


def _complex_rows(re_ref, im_ref, col):
    rows = pl.ds(col, GRID_W, stride=PITCH)
    return jnp.concatenate([re_ref[0, rows, :].astype(BF16), im_ref[0, rows, :].astype(BF16)], axis=0)


def _fin_kernel(x_ref, xn_ref, sh_ref, sc_ref, shn_ref, scn_ref, g_ref, w_ref, u_ref, h_scr):
    t = pl.program_id(0)
    cur = lax.rem(t, 2)

    @pl.when(t == 0)
    def _():
        h_scr[0] = _modulated_norm(x_ref[0], g_ref[...], sc_ref[0], sh_ref[0])

    u = _mm(h_scr[cur], w_ref[...])
    for r in range(x_ref.shape[1] // GRID_W):
        u_ref[0, _slab(r), :] = u[r * GRID_W:(r + 1) * GRID_W]
        u_ref[0, _pad(r), :] = jnp.zeros((PITCH - GRID_W, D), F32)
    h_scr[1 - cur] = _modulated_norm(xn_ref[0], g_ref[...], scn_ref[0], shn_ref[0])


def _fin_call(x, sh, sc, g, w_in):
    B, N, _ = x.shape
    tm = 512
    tiles = N // tm
    nxt = _next_tile(B * tiles)
    row = pl.BlockSpec((1, tm, D), lambda t: (t // tiles, t % tiles, 0))
    row_next = pl.BlockSpec((1, tm, D), lambda t: (nxt(t) // tiles, nxt(t) % tiles, 0))
    vec = pl.BlockSpec((1, 1, D), lambda t: (t // tiles, 0, 0))
    vec_next = pl.BlockSpec((1, 1, D), lambda t: (nxt(t) // tiles, 0, 0))
    return pl.pallas_call(
        _fin_kernel,
        grid=(B * tiles,),
        in_specs=[
            row, row_next, vec, vec, vec_next, vec_next,
            pl.BlockSpec((1, D), lambda t: (0, 0)),
            pl.BlockSpec((D, D), lambda t: (0, 0)),
        ],
        out_specs=pl.BlockSpec((1, tm // GRID_W * PITCH, D), lambda t: (t // tiles, t % tiles, 0)),
        out_shape=jax.ShapeDtypeStruct((B, N_PITCHED, D), F32),
        scratch_shapes=[pltpu.VMEM((2, tm, D), BF16)],
        compiler_params=_cparams(("arbitrary",)),
        name="fin",
    )(x, x, sh, sc, sh, sc, g, w_in)


def _dft_a_kernel(ulo_ref, uhi_ref, cs_ref, m_ref, twc_ref, tws_ref, ar_ref, ai_ref):
    cs, m = cs_ref[...], m_ref[...]
    zero_pad = jnp.zeros((PITCH - GRID_W, GROUP_DIM), F32)
    batch = MXU_TILE // GRID_W
    for col0 in range(0, GRID_W, batch):
        u = []
        for col in range(col0, col0 + batch):
            rows = pl.ds(col, GRID_W, stride=PITCH)
            u.append(jnp.concatenate([ulo_ref[0, rows, :], uhi_ref[0, rows, :]], axis=1).astype(BF16))
        zb = _mm(jnp.concatenate(u, axis=0), cs)
        for i, col in enumerate(range(col0, col0 + batch)):
            z = zb[i * GRID_W:(i + 1) * GRID_W]
            z = jnp.concatenate([z[:, :GROUP_DIM], z[:, GROUP_DIM:]], axis=0).astype(BF16)
            a = _mm(m, z)
            ar, ai = a[:GRID_W], a[GRID_W:]
            c = jnp.concatenate([twc_ref[col]] * (GROUP_DIM // LANES), axis=1)
            s = jnp.concatenate([tws_ref[col]] * (GROUP_DIM // LANES), axis=1)
            ar_ref[0, _slab(col), :] = ar * c + ai * s
            ai_ref[0, _slab(col), :] = ai * c - ar * s
            ar_ref[0, _pad(col), :] = zero_pad
            ai_ref[0, _pad(col), :] = zero_pad


def _dft_a_call(u, cs, m_a, twc, tws):
    B = u.shape[0]
    halves = GROUP_DIM // LANES
    assert halves == 2
    lane_blk = lambda h: pl.BlockSpec((1, N_PITCHED, LANES), lambda b, i: (b, 0, halves * i + h))
    out = pl.BlockSpec((1, N_PITCHED, GROUP_DIM), lambda b, i: (b, 0, i))
    tw = pl.BlockSpec((GRID_W, GRID_W, LANES), lambda b, i: (0, 0, 0))
    shp = jax.ShapeDtypeStruct((B, N_PITCHED, D), F32)
    return pl.pallas_call(
        _dft_a_kernel,
        grid=(B, GROUPS),
        in_specs=[
            lane_blk(0), lane_blk(1),
            pl.BlockSpec((GROUP_DIM, 2 * GROUP_DIM), lambda b, i: (0, 0)),
            pl.BlockSpec((2 * GRID_W, 2 * GRID_W), lambda b, i: (0, 0)),
            tw, tw,
        ],
        out_specs=[out, out],
        out_shape=[shp, shp],
        compiler_params=_cparams(("parallel", "parallel")),
        name="dft_a",
    )(u, u, cs, m_a, twc, tws)


def _dft_c_kernel(ar_ref, ai_ref, m_ref, f_ref):
    m = m_ref[...]
    for kr in range(GRID_W):
        f = _mm(m, _complex_rows(ar_ref, ai_ref, kr))
        f_ref[0, pl.ds(kr, GRID_W, stride=PITCH), :] = f
    for j in range(GRID_W, PITCH):
        f_ref[0, pl.ds(j, GRID_W, stride=PITCH), :] = jnp.zeros((GRID_W, f_ref.shape[2]), F32)


def _dft_c_call(ar, ai, m_c):
    B = ar.shape[0]
    blk = pl.BlockSpec((1, N_PITCHED, LANES), lambda b, i: (b, 0, i))
    return pl.pallas_call(
        _dft_c_kernel,
        grid=(B, D // LANES),
        in_specs=[blk, blk, pl.BlockSpec((GRID_W, 2 * GRID_W), lambda b, i: (0, 0))],
        out_specs=blk,
        out_shape=jax.ShapeDtypeStruct((B, N_PITCHED, D), F32),
        compiler_params=_cparams(("parallel", "parallel")),
        name="dft_c",
    )(ar, ai, m_c)


def _rope_tables(n_tokens):
    pos = jnp.arange(n_tokens)
    lane = jnp.arange(LANES)
    d = lane % HEAD_DIM
    axis = d // (2 * ROPE_FREQS)
    second_half = (d % (2 * ROPE_FREQS)) >= ROPE_FREQS
    inv_freq = ROPE_THETA ** (-(d % ROPE_FREQS).astype(F32) / ROPE_FREQS)
    coord = jnp.where(axis[None, :] == 0, (pos // GRID_W)[:, None], (pos % GRID_W)[:, None])
    ang = coord.astype(F32) * inv_freq[None, :]
    c, s = jnp.cos(ang), jnp.sin(ang)
    return c, jnp.where(second_half[None, :], 0.0, -s), jnp.where(second_half[None, :], s, 0.0)


def _dft_tables():
    two_pi = 2.0 * math.pi
    ch = jnp.arange(GROUP_DIM)
    ang_c = two_pi * ((ch[:, None] * ch[None, :]) % GROUP_DIM).astype(F32) / GROUP_DIM
    ch_scale = GROUP_DIM ** -0.5
    cs = jnp.concatenate([jnp.cos(ang_c), -jnp.sin(ang_c)], axis=1) * ch_scale
    r = jnp.arange(GRID_W)
    ang_r = two_pi * ((r[:, None] * r[None, :]) % GRID_W).astype(F32) / GRID_W
    cr, sr = jnp.cos(ang_r) / 8.0, jnp.sin(ang_r) / 8.0
    m_a = jnp.concatenate([jnp.concatenate([cr, sr], axis=1),
                           jnp.concatenate([-sr, cr], axis=1)], axis=0)
    m_c = jnp.concatenate([cr, sr], axis=1)
    ang_t = two_pi * (r[:, None] * r[None, :]).astype(F32) / (GRID_W * GRID_W)
    twc = jnp.broadcast_to(jnp.cos(ang_t)[:, :, None], (GRID_W, GRID_W, LANES))
    tws = jnp.broadcast_to(jnp.sin(ang_t)[:, :, None], (GRID_W, GRID_W, LANES))
    return cs.astype(BF16), m_a.astype(BF16), m_c.astype(BF16), twc, tws


def kernel(x, c, ctx, c_ctx, mod_w, mod_b, norm_g, ffn_w_gu, ffn_w_down, attn_w_qkv, attn_w_o,
           attn_lambda, attn_subln_g, four_w_in, four_w_out):
    B, N, _ = x.shape
    Lc = ctx.shape[1]
    assert x.shape == (B, GRID_W * GRID_W, D) and ctx.shape == (B, Lc, D) and N % Lc == 0

    cond = jnp.concatenate([c, c_ctx[None, :], jnp.zeros((8 - B - 1, D), F32)], axis=0)
    mods = _ada_call(cond, mod_w, mod_b)

    def lat(layer, idx):
        return mods[layer, :B, idx * D:(idx + 1) * D].reshape(B, 1, D)

    def ctx_mod(layer, idx):
        return mods[layer, B:B + 1, idx * D:(idx + 1) * D]

    g = norm_g.reshape(norm_g.shape[0], 4, 1, D)

    lambda_init = 0.8 - 0.6 * math.exp(-0.3 * 0)
    rope_c, rope_slo, rope_shi = _rope_tables(N)
    w_qkv = attn_w_qkv[0].astype(BF16)
    q, k_lat, vt_lat = _qkv_call(x, lat(0, 0), lat(0, 1), g[0, 0], w_qkv, rope_c, rope_slo, rope_shi)
    k_ctx, vt_ctx = _kv_ctx_call(ctx, ctx_mod(0, 0), ctx_mod(0, 1), g[0, 0], w_qkv)
    later = [ffn_w_gu.reshape(-1, 2 * FFN), ffn_w_down.reshape(-1, D), attn_w_o[0], four_w_in[0], four_w_out[0]]
    o, w_gu, w_down, w_o, w_in, w_out = _attn_call(
        attn_lambda[0], attn_subln_g[0], q, k_lat, vt_lat, k_ctx, vt_ctx, lambda_init, later)
    w_gu, w_down = w_gu.reshape(ffn_w_gu.shape), w_down.reshape(ffn_w_down.shape)
    x = _post_call(o, w_o, x, lat(0, 2), g[0, 1])
    x = _ffn_call(x, lat(0, 3), lat(0, 4), lat(0, 5), g[0, 2], g[0, 3], w_gu, w_down, 0)

    cs, m_a, m_c, twc, tws = _dft_tables()
    u = _fin_call(x, lat(1, 0), lat(1, 1), g[1, 0], w_in)
    f = _dft_c_call(*_dft_a_call(u, cs, m_a, twc, tws), m_c)
    x = _post_call(f, w_out, x, lat(1, 2), g[1, 1], pitched=True)
    x = _ffn_call(x, lat(1, 3), lat(1, 4), lat(1, 5), g[1, 2], g[1, 3], w_gu, w_down, 1)
    return x
```

```python
import functools
import math

import jax
import jax.numpy as jnp
from jax import lax
from jax.experimental import pallas as pl
from jax.experimental.pallas import tpu as pltpu

D = 2048
HEAD_DIM = 64
VALUE_DIM = 2 * HEAD_DIM
N_HEADS = D // VALUE_DIM
GRID_W = 64
ROPE_FREQS = HEAD_DIM // 4
ROPE_THETA = 10000.0
GROUPS = 8
GROUP_DIM = D // GROUPS
FFN = ((8 * D // 3 + 255) // 256) * 256
EPS = 1e-6
LANES = 128
SUBLANES = 8
MXU_TILE = 256
KEY_CHUNK = 2 * MXU_TILE
VT_ROWS = VALUE_DIM + 16
Q_SCALE = HEAD_DIM ** -0.5 * math.log2(math.e)
Q_TILE = MXU_TILE
Q_TILES_PER_STEP = 4
SCORE_LOOKAHEAD = 4
VMEM_LIMIT = 56 * 1024 * 1024

F32 = jnp.float32
BF16 = jnp.bfloat16


def _cparams(sem):
    return pltpu.CompilerParams(dimension_semantics=sem, vmem_limit_bytes=VMEM_LIMIT)


def _rms(x, g):
    return x * lax.rsqrt(jnp.mean(x * x, axis=-1, keepdims=True) + EPS) * g


def _mm(a, b):
    return jnp.dot(a, b, preferred_element_type=F32)


def _modulated_norm(x, g, sc, sh):
    return (_rms(x, g) * (1.0 + sc) + sh).astype(BF16)


def _lookahead_rows(step, n_steps, tm):
    rows = -(-tm // n_steps)
    rows += -rows % 16
    start = jnp.minimum(step * rows, tm - rows)
    return pl.ds(pl.multiple_of(start, 16), rows)


def _next_tile(n_tiles):
    return lambda t: jnp.minimum(t + 1, n_tiles - 1)


def _ada_kernel(c_ref, w_ref, b_ref, o_ref):
    c = c_ref[...]
    s = (c * jax.nn.sigmoid(c)).astype(BF16)
    o_ref[0] = _mm(s, w_ref[0].astype(BF16)) + b_ref[0]


def _ada_call(cond, mod_w, mod_b):
    depth = mod_w.shape[0]
    tn = 1024
    return pl.pallas_call(
        _ada_kernel,
        grid=(depth, 6 * D // tn),
        in_specs=[
            pl.BlockSpec((8, D), lambda l, n: (0, 0)),
            pl.BlockSpec((1, D, tn), lambda l, n: (l, 0, n)),
            pl.BlockSpec((1, 1, tn), lambda l, n: (l, 0, n)),
        ],
        out_specs=pl.BlockSpec((1, 8, tn), lambda l, n: (l, 0, n)),
        out_shape=jax.ShapeDtypeStruct((depth, 8, 6 * D), F32),
        compiler_params=_cparams(("parallel", "parallel")),
        name="ada",
    )(cond, mod_w, mod_b.reshape(depth, 1, 6 * D))


def _rope(t, c, s_lo, s_hi):
    return t * c + pltpu.roll(t, 16, 1) * s_hi + pltpu.roll(t, LANES - 16, 1) * s_lo


def _store_vt(vt_ref, j, v):
    n = v.shape[0]
    row = lax.broadcasted_iota(jnp.int32, (VT_ROWS - VALUE_DIM, n), 0)
    extra = jnp.where(row == 0, 1.0, 0.0).astype(BF16)
    vt_ref[0, j] = jnp.concatenate([v.T.astype(BF16), extra], axis=0)


def _qkv_kernel(x_ref, xn_ref, sh_ref, sc_ref, shn_ref, scn_ref, g_ref, wq_ref, wk_ref, wv_ref,
                c_ref, slo_ref, shi_ref, q_ref, k_ref, vt_ref, h_scr, *, heads_per_step, n_steps):
    t, n = pl.program_id(0), pl.program_id(1)
    cur = lax.rem(t, 2)

    @pl.when((t == 0) & (n == 0))
    def _():
        h_scr[0] = _modulated_norm(x_ref[0], g_ref[...], sc_ref[0], sh_ref[0])

    h = h_scr[cur]
    q = _mm(h, wq_ref[...])
    k = _mm(h, wk_ref[...])
    v = _mm(h, wv_ref[...])
    c, s_lo, s_hi = c_ref[...], slo_ref[...], shi_ref[...]
    for j in range(heads_per_step):
        sl = slice(j * LANES, (j + 1) * LANES)
        q_ref[0, j] = (_rope(q[:, sl], c, s_lo, s_hi) * Q_SCALE).astype(BF16)
        k_ref[0, j] = _rope(k[:, sl], c, s_lo, s_hi).astype(BF16)
        _store_vt(vt_ref, j, v[:, sl])

    rows = _lookahead_rows(n, n_steps, x_ref.shape[1])
    h_scr[1 - cur, rows, :] = _modulated_norm(xn_ref[0, rows, :], g_ref[...], scn_ref[0], shn_ref[0])


def _qkv_call(x, sh, sc, g, w_qkv, rope_c, rope_slo, rope_shi):
    B, N, _ = x.shape
    tm, tn = 512, 512
    hps = tn // LANES
    n_steps = D // tn
    tiles = N // tm
    nxt = _next_tile(B * tiles)
    row = pl.BlockSpec((1, tm, D), lambda t, n: (t // tiles, t % tiles, 0))
    row_next = pl.BlockSpec((1, tm, D), lambda t, n: (nxt(t) // tiles, nxt(t) % tiles, 0))
    vec = pl.BlockSpec((1, 1, D), lambda t, n: (t // tiles, 0, 0))
    vec_next = pl.BlockSpec((1, 1, D), lambda t, n: (nxt(t) // tiles, 0, 0))
    tab = pl.BlockSpec((tm, LANES), lambda t, n: (t % tiles, 0))
    out = pl.BlockSpec((1, hps, tm, LANES), lambda t, n: (t // tiles, n, t % tiles, 0))
    return pl.pallas_call(
        functools.partial(_qkv_kernel, heads_per_step=hps, n_steps=n_steps),
        grid=(B * tiles, n_steps),
        in_specs=[
            row, row_next, vec, vec, vec_next, vec_next,
            pl.BlockSpec((1, D), lambda t, n: (0, 0)),
            pl.BlockSpec((D, tn), lambda t, n: (0, n)),
            pl.BlockSpec((D, tn), lambda t, n: (0, n + n_steps)),
            pl.BlockSpec((D, tn), lambda t, n: (0, n + 2 * n_steps)),
            tab, tab, tab,
        ],
        out_specs=[out, out,
                   pl.BlockSpec((1, hps, VT_ROWS, tm), lambda t, n: (t // tiles, n, 0, t % tiles))],
        out_shape=[jax.ShapeDtypeStruct((B, N_HEADS, N, VALUE_DIM), BF16),
                   jax.ShapeDtypeStruct((B, N_HEADS, N, VALUE_DIM), BF16),
                   jax.ShapeDtypeStruct((B, N_HEADS, VT_ROWS, N), BF16)],
        scratch_shapes=[pltpu.VMEM((2, tm, D), BF16)],
        compiler_params=_cparams(("arbitrary", "arbitrary")),
        name="qkv",
    )(x, x, sh, sc, sh, sc, g, w_qkv, w_qkv, w_qkv, rope_c, rope_slo, rope_shi)


def _kv_ctx_kernel(x_ref, sh_ref, sc_ref, g_ref, wk_ref, wv_ref, k_ref, vt_ref, h_scr, *, heads_per_step):
    @pl.when(pl.program_id(1) == 0)
    def _():
        h_scr[...] = _modulated_norm(x_ref[0], g_ref[...], sc_ref[...], sh_ref[...])

    h = h_scr[...]
    k = _mm(h, wk_ref[...])
    v = _mm(h, wv_ref[...])
    for j in range(heads_per_step):
        sl = slice(j * LANES, (j + 1) * LANES)
        k_ref[0, j] = k[:, sl].astype(BF16)
        _store_vt(vt_ref, j, v[:, sl])


def _kv_ctx_call(ctx, sh, sc, g, w_qkv):
    B, Lc, _ = ctx.shape
    assert Lc % LANES == 0
    tn = 512
    hps = tn // LANES
    n_steps = D // tn
    vec = pl.BlockSpec((1, D), lambda b, n: (0, 0))
    return pl.pallas_call(
        functools.partial(_kv_ctx_kernel, heads_per_step=hps),
        grid=(B, n_steps),
        in_specs=[
            pl.BlockSpec((1, Lc, D), lambda b, n: (b, 0, 0)),
            vec, vec, vec,
            pl.BlockSpec((D, tn), lambda b, n: (0, n + n_steps)),
            pl.BlockSpec((D, tn), lambda b, n: (0, n + 2 * n_steps)),
        ],
        out_specs=[pl.BlockSpec((1, hps, Lc, LANES), lambda b, n: (b, n, 0, 0)),
                   pl.BlockSpec((1, hps, VT_ROWS, Lc), lambda b, n: (b, n, 0, 0))],
        out_shape=[jax.ShapeDtypeStruct((B, N_HEADS, Lc, VALUE_DIM), BF16),
                   jax.ShapeDtypeStruct((B, N_HEADS, VT_ROWS, Lc), BF16)],
        scratch_shapes=[pltpu.VMEM((Lc, D), BF16)],
        compiler_params=_cparams(("parallel", "arbitrary")),
        name="kv_ctx",
    )(ctx, sh, sc, g, w_qkv, w_qkv)


def _sublane_fold(x, op):
    return op(x.reshape(x.shape[0] // SUBLANES, SUBLANES, x.shape[1]), axis=0)


def _attn_kernel(lam_ref, sg_ref, q_ref, kl_ref, vtl_ref, kc_ref, vtc_ref, *rest, lambda_init, chunks):
    n_cast = (len(rest) - 2) // 2
    o_ref, pv_scr = rest[n_cast], rest[-1]
    for w_ref, w_out in zip(rest[:n_cast], rest[n_cast + 1:-1]):
        w_out[...] = w_ref[...].astype(BF16)
    lf = lam_ref[...]
    lam = (jnp.exp(jnp.sum(lf[0:1] * lf[1:2], axis=-1, keepdims=True))
           - jnp.exp(jnp.sum(lf[2:3] * lf[3:4], axis=-1, keepdims=True)) + lambda_init)
    n_tiles = q_ref.shape[2] // Q_TILE
    lane = lax.broadcasted_iota(jnp.int32, (Q_TILE, VALUE_DIM), 1)
    zero = jnp.zeros((Q_TILE, VALUE_DIM), BF16)
    q_maps = []
    for ti in range(n_tiles):
        q = q_ref[0, 0, ti * Q_TILE:(ti + 1) * Q_TILE, :]
        q_maps.append((jnp.where(lane < HEAD_DIM, q, zero), jnp.where(lane >= HEAD_DIM, q, zero)))
    dn = (((1,), (1,)), ((), ()))
    tasks = [(ti, ci, mp) for ti in range(n_tiles) for ci in range(len(chunks)) for mp in range(2)]

    k_refs, vt_refs = (kl_ref, kc_ref), (vtl_ref, vtc_ref)

    def scores(t):
        ti, ci, mp = tasks[t]
        src, lo, hi = chunks[ci]
        return lax.dot_general(k_refs[src][0, 0, lo:hi, :], q_maps[ti][mp], dn,
                               preferred_element_type=F32)

    def merge(ti, maxes):
        ot = None
        for mp in range(2):
            m = functools.reduce(jnp.maximum, maxes[mp])
            w = [jnp.exp2(mc - m) for mc in maxes[mp]]
            den = functools.reduce(
                jnp.add, [wc * pv_scr[ti, mp, ci, VALUE_DIM:VALUE_DIM + 1, :] for ci, wc in enumerate(w)])
            norm = 1.0 / den if mp == 0 else -lam / den
            for ci, wc in enumerate(w):
                term = pv_scr[ti, mp, ci, :VALUE_DIM, :] * (wc * norm)
                ot = term if ot is None else ot + term
        ot = ot * lax.rsqrt(jnp.mean(ot * ot, axis=0, keepdims=True) + EPS)
        o_ref[0, ti * Q_TILE:(ti + 1) * Q_TILE, :] = (
            ot.T * (sg_ref[...] * (1.0 - lambda_init))).astype(BF16)

    pending = [scores(t) for t in range(SCORE_LOOKAHEAD)]
    maxes = ([], [])
    for t, (ti, ci, mp) in enumerate(tasks):
        if t + SCORE_LOOKAHEAD < len(tasks):
            pending.append(scores(t + SCORE_LOOKAHEAD))
        s = pending.pop(0)
        mc = jnp.max(_sublane_fold(s, jnp.max), axis=0, keepdims=True)
        e = jnp.exp2(s - mc).astype(BF16)
        maxes[mp].append(mc)
        src, lo, hi = chunks[ci]
        pv_scr[ti, mp, ci] = _mm(vt_refs[src][0, 0, :, lo:hi], e)
        if (ci, mp) == (len(chunks) - 1, 1):
            merge(ti, maxes)
            maxes = ([], [])


def _key_chunks(src, n_keys):
    assert n_keys % MXU_TILE == 0
    return tuple((src, lo, min(lo + KEY_CHUNK, n_keys)) for lo in range(0, n_keys, KEY_CHUNK))


def _cast_rows(n_rows, n_steps):
    rows = next(r for r in range(16, n_rows + 1, 16) if n_rows % r == 0 and r * n_steps >= n_rows)
    return rows, n_rows // rows


def _attn_call(lam, subln_g, q, k_lat, vt_lat, k_ctx, vt_ctx, lambda_init, weights):
    B, H, N, _ = q.shape
    n_ctx = k_ctx.shape[2]
    chunks = _key_chunks(0, N) + _key_chunks(1, n_ctx)
    tq = Q_TILE * Q_TILES_PER_STEP
    n_q = N // tq
    per_head = lambda b, h, i: (b, h, 0, 0)

    def cast_spec(w):
        rows, n_blocks = _cast_rows(w.shape[0], B * H * n_q)
        return pl.BlockSpec((rows, w.shape[1]),
                            lambda b, h, i: (jnp.minimum((b * H + h) * n_q + i, n_blocks - 1), 0))

    return pl.pallas_call(
        functools.partial(_attn_kernel, lambda_init=lambda_init, chunks=chunks),
        grid=(B, H, n_q),
        in_specs=[
            pl.BlockSpec((4, HEAD_DIM), lambda b, h, i: (0, 0)),
            pl.BlockSpec((1, VALUE_DIM), lambda b, h, i: (0, 0)),
            pl.BlockSpec((1, 1, tq, VALUE_DIM), lambda b, h, i: (b, h, i, 0)),
            pl.BlockSpec((1, 1, N, VALUE_DIM), per_head),
            pl.BlockSpec((1, 1, VT_ROWS, N), per_head),
            pl.BlockSpec((1, 1, n_ctx, VALUE_DIM), per_head),
            pl.BlockSpec((1, 1, VT_ROWS, n_ctx), per_head),
            *[cast_spec(w) for w in weights],
        ],
        out_specs=[pl.BlockSpec((1, tq, VALUE_DIM), lambda b, h, i: (b, i, h)),
                   *[cast_spec(w) for w in weights]],
        out_shape=[jax.ShapeDtypeStruct((B, N, D), BF16),
                   *[jax.ShapeDtypeStruct(w.shape, BF16) for w in weights]],
        scratch_shapes=[pltpu.VMEM((Q_TILES_PER_STEP, 2, len(chunks), VT_ROWS, Q_TILE), F32)],
        compiler_params=_cparams(("arbitrary", "arbitrary", "arbitrary")),
        name="attn",
    )(lam, subln_g.reshape(1, VALUE_DIM), q, k_lat, vt_lat, k_ctx, vt_ctx, *weights)


def _post_kernel(a_ref, w_ref, x_ref, gt_ref, g_ref, o_ref, *, pitched):
    if pitched:
        a = jnp.concatenate([a_ref[0, _slab(r), :] for r in range(a_ref.shape[1] // PITCH)], axis=0)
    else:
        a = a_ref[0]
    y = _mm(a.astype(BF16), w_ref[...])
    o_ref[0] = x_ref[0] + gt_ref[0] * _rms(y, g_ref[...])


def _post_call(a, w, x, gt, g, pitched=False):
    B, N, _ = x.shape
    tm = 512
    row = lambda b, m: (b, m, 0)
    return pl.pallas_call(
        functools.partial(_post_kernel, pitched=pitched),
        grid=(B, N // tm),
        in_specs=[
            pl.BlockSpec((1, tm // GRID_W * PITCH if pitched else tm, D), row),
            pl.BlockSpec((D, D), lambda b, m: (0, 0)),
            pl.BlockSpec((1, tm, D), row),
            pl.BlockSpec((1, 1, D), lambda b, m: (b, 0, 0)),
            pl.BlockSpec((1, D), lambda b, m: (0, 0)),
        ],
        out_specs=pl.BlockSpec((1, tm, D), row),
        out_shape=jax.ShapeDtypeStruct(x.shape, F32),
        compiler_params=_cparams(("parallel", "parallel")),
        name="post",
    )(a, w, x, gt, g)


def _ffn_kernel(xp_ref, xn_ref, shp_ref, scp_ref, shn_ref, scn_ref, gtp_ref, g_in_ref, g_out_ref,
                wg_ref, wu_ref, wd_ref, o_ref, h_scr, acc_scr, *, n_chunks, n_tiles):
    t, j = pl.program_id(0), pl.program_id(1)
    cur = lax.rem(t, 2)
    tm = xp_ref.shape[1]

    @pl.when((t == 0) & (j == 0))
    def _():
        h_scr[0] = _modulated_norm(xp_ref[0], g_in_ref[...], scp_ref[0], shp_ref[0])
        acc_scr[1] = jnp.zeros((tm, D), F32)

    @pl.when(t < n_tiles)
    def _():
        @pl.when(j == 0)
        def _():
            acc_scr[cur] = jnp.zeros((tm, D), F32)

        rows = _lookahead_rows(j, n_chunks, tm)
        o_ref[0, rows, :] = xp_ref[0, rows, :] + gtp_ref[0] * _rms(acc_scr[1 - cur, rows, :], g_out_ref[...])

        h = h_scr[cur]
        g = _mm(h, wg_ref[...])
        u = _mm(h, wu_ref[...])
        a = (g * jax.nn.sigmoid(g) * u).astype(BF16)
        acc_scr[cur] += _mm(a, wd_ref[...])

        h_scr[1 - cur, rows, :] = _modulated_norm(xn_ref[0, rows, :], g_in_ref[...], scn_ref[0], shn_ref[0])

    @pl.when((t == n_tiles) & (j == 0))
    def _():
        o_ref[0] = xp_ref[0] + gtp_ref[0] * _rms(acc_scr[1 - cur], g_out_ref[...])


def _ffn_call(x, sh, sc, gt, g_in, g_out, w_gu, w_down, layer):
    B, N, _ = x.shape
    tm, th = 512, 512
    n_chunks = FFN // th
    tiles = N // tm
    n_tiles = B * tiles

    def prev(t):
        return jnp.maximum(t - 1, 0)

    def nxt(t):
        return jnp.minimum(t + 1, n_tiles - 1)

    def chunk(t, j):
        return jnp.where(t == n_tiles, n_chunks - 1, j)

    row_prev = pl.BlockSpec((1, tm, D), lambda t, j: (prev(t) // tiles, prev(t) % tiles, 0))
    row_next = pl.BlockSpec((1, tm, D), lambda t, j: (nxt(t) // tiles, nxt(t) % tiles, 0))
    vec_prev = pl.BlockSpec((1, 1, D), lambda t, j: (prev(t) // tiles, 0, 0))
    vec_next = pl.BlockSpec((1, 1, D), lambda t, j: (nxt(t) // tiles, 0, 0))
    gsp = pl.BlockSpec((1, D), lambda t, j: (0, 0))
    return pl.pallas_call(
        functools.partial(_ffn_kernel, n_chunks=n_chunks, n_tiles=n_tiles),
        grid=(n_tiles + 1, n_chunks),
        in_specs=[
            row_prev, row_next, vec_prev, vec_prev, vec_next, vec_next, vec_prev, gsp, gsp,
            pl.BlockSpec((None, D, th), lambda t, j: (layer, 0, chunk(t, j))),
            pl.BlockSpec((None, D, th), lambda t, j: (layer, 0, chunk(t, j) + n_chunks)),
            pl.BlockSpec((None, th, D), lambda t, j: (layer, chunk(t, j), 0)),
        ],
        out_specs=row_prev,
        out_shape=jax.ShapeDtypeStruct(x.shape, F32),
        scratch_shapes=[pltpu.VMEM((2, tm, D), BF16), pltpu.VMEM((2, tm, D), F32)],
        compiler_params=_cparams(("arbitrary", "arbitrary")),
        name="ffn",
    )(x, x, sh, sc, sh, sc, gt, g_in, g_out, w_gu, w_gu, w_down)


PITCH = GRID_W + SUBLANES
N_PITCHED = GRID_W * PITCH


def _slab(i):
    return slice(i * PITCH, i * PITCH + GRID_W)


def _pad(i):
    return slice(i * PITCH + GRID_W, (i + 1) * PITCH)


def _complex_rows(re_ref, im_ref, col):
    rows = pl.ds(col, GRID_W, stride=PITCH)
    return jnp.concatenate([re_ref[0, rows, :].astype(BF16), im_ref[0, rows, :].astype(BF16)], axis=0)


def _fin_kernel(x_ref, xn_ref, sh_ref, sc_ref, shn_ref, scn_ref, g_ref, w_ref, u_ref, h_scr):
    t = pl.program_id(0)
    cur = lax.rem(t, 2)

    @pl.when(t == 0)
    def _():
        h_scr[0] = _modulated_norm(x_ref[0], g_ref[...], sc_ref[0], sh_ref[0])

    u = _mm(h_scr[cur], w_ref[...])
    for r in range(x_ref.shape[1] // GRID_W):
        u_ref[0, _slab(r), :] = u[r * GRID_W:(r + 1) * GRID_W]
        u_ref[0, _pad(r), :] = jnp.zeros((PITCH - GRID_W, D), F32)
    h_scr[1 - cur] = _modulated_norm(xn_ref[0], g_ref[...], scn_ref[0], shn_ref[0])


def _fin_call(x, sh, sc, g, w_in):
    B, N, _ = x.shape
    tm = 512
    tiles = N // tm
    nxt = _next_tile(B * tiles)
    row = pl.BlockSpec((1, tm, D), lambda t: (t // tiles, t % tiles, 0))
    row_next = pl.BlockSpec((1, tm, D), lambda t: (nxt(t) // tiles, nxt(t) % tiles, 0))
    vec = pl.BlockSpec((1, 1, D), lambda t: (t // tiles, 0, 0))
    vec_next = pl.BlockSpec((1, 1, D), lambda t: (nxt(t) // tiles, 0, 0))
    return pl.pallas_call(
        _fin_kernel,
        grid=(B * tiles,),
        in_specs=[
            row, row_next, vec, vec, vec_next, vec_next,
            pl.BlockSpec((1, D), lambda t: (0, 0)),
            pl.BlockSpec((D, D), lambda t: (0, 0)),
        ],
        out_specs=pl.BlockSpec((1, tm // GRID_W * PITCH, D), lambda t: (t // tiles, t % tiles, 0)),
        out_shape=jax.ShapeDtypeStruct((B, N_PITCHED, D), F32),
        scratch_shapes=[pltpu.VMEM((2, tm, D), BF16)],
        compiler_params=_cparams(("arbitrary",)),
        name="fin",
    )(x, x, sh, sc, sh, sc, g, w_in)


def _dft_a_kernel(ulo_ref, uhi_ref, cs_ref, m_ref, twc_ref, tws_ref, ar_ref, ai_ref):
    cs, m = cs_ref[...], m_ref[...]
    zero_pad = jnp.zeros((PITCH - GRID_W, GROUP_DIM), F32)
    batch = MXU_TILE // GRID_W
    for col0 in range(0, GRID_W, batch):
        u = []
        for col in range(col0, col0 + batch):
            rows = pl.ds(col, GRID_W, stride=PITCH)
            u.append(jnp.concatenate([ulo_ref[0, rows, :], uhi_ref[0, rows, :]], axis=1).astype(BF16))
        zb = _mm(jnp.concatenate(u, axis=0), cs)
        for i, col in enumerate(range(col0, col0 + batch)):
            z = zb[i * GRID_W:(i + 1) * GRID_W]
            z = jnp.concatenate([z[:, :GROUP_DIM], z[:, GROUP_DIM:]], axis=0).astype(BF16)
            a = _mm(m, z)
            ar, ai = a[:GRID_W], a[GRID_W:]
            c = jnp.concatenate([twc_ref[col]] * (GROUP_DIM // LANES), axis=1)
            s = jnp.concatenate([tws_ref[col]] * (GROUP_DIM // LANES), axis=1)
            ar_ref[0, _slab(col), :] = ar * c + ai * s
            ai_ref[0, _slab(col), :] = ai * c - ar * s
            ar_ref[0, _pad(col), :] = zero_pad
            ai_ref[0, _pad(col), :] = zero_pad


def _dft_a_call(u, cs, m_a, twc, tws):
    B = u.shape[0]
    halves = GROUP_DIM // LANES
    assert halves == 2
    lane_blk = lambda h: pl.BlockSpec((1, N_PITCHED, LANES), lambda b, i: (b, 0, halves * i + h))
    out = pl.BlockSpec((1, N_PITCHED, GROUP_DIM), lambda b, i: (b, 0, i))
    tw = pl.BlockSpec((GRID_W, GRID_W, LANES), lambda b, i: (0, 0, 0))
    shp = jax.ShapeDtypeStruct((B, N_PITCHED, D), F32)
    return pl.pallas_call(
        _dft_a_kernel,
        grid=(B, GROUPS),
        in_specs=[
            lane_blk(0), lane_blk(1),
            pl.BlockSpec((GROUP_DIM, 2 * GROUP_DIM), lambda b, i: (0, 0)),
            pl.BlockSpec((2 * GRID_W, 2 * GRID_W), lambda b, i: (0, 0)),
            tw, tw,
        ],
        out_specs=[out, out],
        out_shape=[shp, shp],
        compiler_params=_cparams(("parallel", "parallel")),
        name="dft_a",
    )(u, u, cs, m_a, twc, tws)


def _dft_c_kernel(ar_ref, ai_ref, m_ref, f_ref):
    m = m_ref[...]
    for kr in range(GRID_W):
        f = _mm(m, _complex_rows(ar_ref, ai_ref, kr))
        f_ref[0, pl.ds(kr, GRID_W, stride=PITCH), :] = f
    for j in range(GRID_W, PITCH):
        f_ref[0, pl.ds(j, GRID_W, stride=PITCH), :] = jnp.zeros((GRID_W, f_ref.shape[2]), F32)


def _dft_c_call(ar, ai, m_c):
    B = ar.shape[0]
    blk = pl.BlockSpec((1, N_PITCHED, LANES), lambda b, i: (b, 0, i))
    return pl.pallas_call(
        _dft_c_kernel,
        grid=(B, D // LANES),
        in_specs=[blk, blk, pl.BlockSpec((GRID_W, 2 * GRID_W), lambda b, i: (0, 0))],
        out_specs=blk,
        out_shape=jax.ShapeDtypeStruct((B, N_PITCHED, D), F32),
        compiler_params=_cparams(("parallel", "parallel")),
        name="dft_c",
    )(ar, ai, m_c)


def _rope_tables(n_tokens):
    pos = jnp.arange(n_tokens)
    lane = jnp.arange(LANES)
    d = lane % HEAD_DIM
    axis = d // (2 * ROPE_FREQS)
    second_half = (d % (2 * ROPE_FREQS)) >= ROPE_FREQS
    inv_freq = ROPE_THETA ** (-(d % ROPE_FREQS).astype(F32) / ROPE_FREQS)
    coord = jnp.where(axis[None, :] == 0, (pos // GRID_W)[:, None], (pos % GRID_W)[:, None])
    ang = coord.astype(F32) * inv_freq[None, :]
    c, s = jnp.cos(ang), jnp.sin(ang)
    return c, jnp.where(second_half[None, :], 0.0, -s), jnp.where(second_half[None, :], s, 0.0)


def _dft_tables():
    two_pi = 2.0 * math.pi
    ch = jnp.arange(GROUP_DIM)
    ang_c = two_pi * ((ch[:, None] * ch[None, :]) % GROUP_DIM).astype(F32) / GROUP_DIM
    ch_scale = GROUP_DIM ** -0.5
    cs = jnp.concatenate([jnp.cos(ang_c), -jnp.sin(ang_c)], axis=1) * ch_scale
    r = jnp.arange(GRID_W)
    ang_r = two_pi * ((r[:, None] * r[None, :]) % GRID_W).astype(F32) / GRID_W
    cr, sr = jnp.cos(ang_r) / 8.0, jnp.sin(ang_r) / 8.0
    m_a = jnp.concatenate([jnp.concatenate([cr, sr], axis=1),
                           jnp.concatenate([-sr, cr], axis=1)], axis=0)
    m_c = jnp.concatenate([cr, sr], axis=1)
    ang_t = two_pi * (r[:, None] * r[None, :]).astype(F32) / (GRID_W * GRID_W)
    twc = jnp.broadcast_to(jnp.cos(ang_t)[:, :, None], (GRID_W, GRID_W, LANES))
    tws = jnp.broadcast_to(jnp.sin(ang_t)[:, :, None], (GRID_W, GRID_W, LANES))
    return cs.astype(BF16), m_a.astype(BF16), m_c.astype(BF16), twc, tws


def kernel(x, c, ctx, c_ctx, mod_w, mod_b, norm_g, ffn_w_gu, ffn_w_down, attn_w_qkv, attn_w_o,
           attn_lambda, attn_subln_g, four_w_in, four_w_out):
    B, N, _ = x.shape
    Lc = ctx.shape[1]
    assert x.shape == (B, GRID_W * GRID_W, D) and ctx.shape == (B, Lc, D) and N % Lc == 0

    cond = jnp.concatenate([c, c_ctx[None, :], jnp.zeros((8 - B - 1, D), F32)], axis=0)
    mods = _ada_call(cond, mod_w, mod_b)

    def lat(layer, idx):
        return mods[layer, :B, idx * D:(idx + 1) * D].reshape(B, 1, D)

    def ctx_mod(layer, idx):
        return mods[layer, B:B + 1, idx * D:(idx + 1) * D]

    g = norm_g.reshape(norm_g.shape[0], 4, 1, D)

    lambda_init = 0.8 - 0.6 * math.exp(-0.3 * 0)
    rope_c, rope_slo, rope_shi = _rope_tables(N)
    w_qkv = attn_w_qkv[0].astype(BF16)
    q, k_lat, vt_lat = _qkv_call(x, lat(0, 0), lat(0, 1), g[0, 0], w_qkv, rope_c, rope_slo, rope_shi)
    k_ctx, vt_ctx = _kv_ctx_call(ctx, ctx_mod(0, 0), ctx_mod(0, 1), g[0, 0], w_qkv)
    later = [ffn_w_gu.reshape(-1, 2 * FFN), ffn_w_down.reshape(-1, D), attn_w_o[0], four_w_in[0], four_w_out[0]]
    o, w_gu, w_down, w_o, w_in, w_out = _attn_call(
        attn_lambda[0], attn_subln_g[0], q, k_lat, vt_lat, k_ctx, vt_ctx, lambda_init, later)
    w_gu, w_down = w_gu.reshape(ffn_w_gu.shape), w_down.reshape(ffn_w_down.shape)
    x = _post_call(o, w_o, x, lat(0, 2), g[0, 1])
    x = _ffn_call(x, lat(0, 3), lat(0, 4), lat(0, 5), g[0, 2], g[0, 3], w_gu, w_down, 0)

    cs, m_a, m_c, twc, tws = _dft_tables()
    u = _fin_call(x, lat(1, 0), lat(1, 1), g[1, 0], w_in)
    f = _dft_c_call(*_dft_a_call(u, cs, m_a, twc, tws), m_c)
    x = _post_call(f, w_out, x, lat(1, 2), g[1, 1], pitched=True)
    x = _ffn_call(x, lat(1, 3), lat(1, 4), lat(1, 5), g[1, 2], g[1, 3], w_gu, w_down, 1)
    return x
```

```python
import functools
import math

import jax
import jax.numpy as jnp
from jax import lax
from jax.experimental import pallas as pl
from jax.experimental.pallas import tpu as pltpu

D = 2048
HEAD_DIM = 64
VALUE_DIM = 2 * HEAD_DIM
N_HEADS = D // VALUE_DIM
GRID_W = 64
ROPE_FREQS = HEAD_DIM // 4
ROPE_THETA = 10000.0
GROUPS = 8
GROUP_DIM = D // GROUPS
FFN = ((8 * D // 3 + 255) // 256) * 256
EPS = 1e-6
LANES = 128
SUBLANES = 8
MXU_TILE = 256
KEY_CHUNK = 2 * MXU_TILE
VT_ROWS = VALUE_DIM + 16
Q_SCALE = HEAD_DIM ** -0.5 * math.log2(math.e)
Q_TILE = MXU_TILE
Q_TILES_PER_STEP = 8
SCORE_LOOKAHEAD = 4
VMEM_LIMIT = 56 * 1024 * 1024

F32 = jnp.float32
BF16 = jnp.bfloat16


def _cparams(sem):
    return pltpu.CompilerParams(dimension_semantics=sem, vmem_limit_bytes=VMEM_LIMIT)


def _rms(x, g):
    return x * lax.rsqrt(jnp.mean(x * x, axis=-1, keepdims=True) + EPS) * g


def _mm(a, b):
    return jnp.dot(a, b, preferred_element_type=F32)


def _modulated_norm(x, g, sc, sh):
    return (_rms(x, g) * (1.0 + sc) + sh).astype(BF16)


def _lookahead_rows(step, n_steps, tm):
    rows = -(-tm // n_steps)
    rows += -rows % 16
    start = jnp.minimum(step * rows, tm - rows)
    return pl.ds(pl.multiple_of(start, 16), rows)


def _next_tile(n_tiles):
    return lambda t: jnp.minimum(t + 1, n_tiles - 1)


def _ada_kernel(c_ref, w_ref, b_ref, o_ref):
    c = c_ref[...]
    s = (c * jax.nn.sigmoid(c)).astype(BF16)
    o_ref[0] = _mm(s, w_ref[0].astype(BF16)) + b_ref[0]


def _ada_call(cond, mod_w, mod_b):
    depth = mod_w.shape[0]
    tn = 1024
    return pl.pallas_call(
        _ada_kernel,
        grid=(depth, 6 * D // tn),
        in_specs=[
            pl.BlockSpec((8, D), lambda l, n: (0, 0)),
            pl.BlockSpec((1, D, tn), lambda l, n: (l, 0, n)),
            pl.BlockSpec((1, 1, tn), lambda l, n: (l, 0, n)),
        ],
        out_specs=pl.BlockSpec((1, 8, tn), lambda l, n: (l, 0, n)),
        out_shape=jax.ShapeDtypeStruct((depth, 8, 6 * D), F32),
        compiler_params=_cparams(("parallel", "parallel")),
        name="ada",
    )(cond, mod_w, mod_b.reshape(depth, 1, 6 * D))


def _rope(t, c, s_lo, s_hi):
    return t * c + pltpu.roll(t, 16, 1) * s_hi + pltpu.roll(t, LANES - 16, 1) * s_lo


def _store_vt(vt_ref, j, v):
    n = v.shape[0]
    row = lax.broadcasted_iota(jnp.int32, (VT_ROWS - VALUE_DIM, n), 0)
    extra = jnp.where(row == 0, 1.0, 0.0).astype(BF16)
    vt_ref[0, j] = jnp.concatenate([v.T.astype(BF16), extra], axis=0)


def _qkv_kernel(x_ref, xn_ref, sh_ref, sc_ref, shn_ref, scn_ref, g_ref, wq_ref, wk_ref, wv_ref,
                c_ref, slo_ref, shi_ref, q_ref, k_ref, vt_ref, h_scr, *, heads_per_step, n_steps):
    t, n = pl.program_id(0), pl.program_id(1)
    cur = lax.rem(t, 2)

    @pl.when((t == 0) & (n == 0))
    def _():
        h_scr[0] = _modulated_norm(x_ref[0], g_ref[...], sc_ref[0], sh_ref[0])

    h = h_scr[cur]
    q = _mm(h, wq_ref[...])
    k = _mm(h, wk_ref[...])
    v = _mm(h, wv_ref[...])
    c, s_lo, s_hi = c_ref[...], slo_ref[...], shi_ref[...]
    for j in range(heads_per_step):
        sl = slice(j * LANES, (j + 1) * LANES)
        q_ref[0, j] = (_rope(q[:, sl], c, s_lo, s_hi) * Q_SCALE).astype(BF16)
        k_ref[0, j] = _rope(k[:, sl], c, s_lo, s_hi).astype(BF16)
        _store_vt(vt_ref, j, v[:, sl])

    rows = _lookahead_rows(n, n_steps, x_ref.shape[1])
    h_scr[1 - cur, rows, :] = _modulated_norm(xn_ref[0, rows, :], g_ref[...], scn_ref[0], shn_ref[0])


def _qkv_call(x, sh, sc, g, w_qkv, rope_c, rope_slo, rope_shi):
    B, N, _ = x.shape
    tm, tn = 512, 512
    hps = tn // LANES
    n_steps = D // tn
    tiles = N // tm
    nxt = _next_tile(B * tiles)
    row = pl.BlockSpec((1, tm, D), lambda t, n: (t // tiles, t % tiles, 0))
    row_next = pl.BlockSpec((1, tm, D), lambda t, n: (nxt(t) // tiles, nxt(t) % tiles, 0))
    vec = pl.BlockSpec((1, 1, D), lambda t, n: (t // tiles, 0, 0))
    vec_next = pl.BlockSpec((1, 1, D), lambda t, n: (nxt(t) // tiles, 0, 0))
    tab = pl.BlockSpec((tm, LANES), lambda t, n: (t % tiles, 0))
    out = pl.BlockSpec((1, hps, tm, LANES), lambda t, n: (t // tiles, n, t % tiles, 0))
    return pl.pallas_call(
        functools.partial(_qkv_kernel, heads_per_step=hps, n_steps=n_steps),
        grid=(B * tiles, n_steps),
        in_specs=[
            row, row_next, vec, vec, vec_next, vec_next,
            pl.BlockSpec((1, D), lambda t, n: (0, 0)),
            pl.BlockSpec((D, tn), lambda t, n: (0, n)),
            pl.BlockSpec((D, tn), lambda t, n: (0, n + n_steps)),
            pl.BlockSpec((D, tn), lambda t, n: (0, n + 2 * n_steps)),
            tab, tab, tab,
        ],
        out_specs=[out, out,
                   pl.BlockSpec((1, hps, VT_ROWS, tm), lambda t, n: (t // tiles, n, 0, t % tiles))],
        out_shape=[jax.ShapeDtypeStruct((B, N_HEADS, N, VALUE_DIM), BF16),
                   jax.ShapeDtypeStruct((B, N_HEADS, N, VALUE_DIM), BF16),
                   jax.ShapeDtypeStruct((B, N_HEADS, VT_ROWS, N), BF16)],
        scratch_shapes=[pltpu.VMEM((2, tm, D), BF16)],
        compiler_params=_cparams(("arbitrary", "arbitrary")),
        name="qkv",
    )(x, x, sh, sc, sh, sc, g, w_qkv, w_qkv, w_qkv, rope_c, rope_slo, rope_shi)


def _kv_ctx_kernel(x_ref, sh_ref, sc_ref, g_ref, wk_ref, wv_ref, k_ref, vt_ref, h_scr, *, heads_per_step):
    @pl.when(pl.program_id(1) == 0)
    def _():
        h_scr[...] = _modulated_norm(x_ref[0], g_ref[...], sc_ref[...], sh_ref[...])

    h = h_scr[...]
    k = _mm(h, wk_ref[...])
    v = _mm(h, wv_ref[...])
    for j in range(heads_per_step):
        sl = slice(j * LANES, (j + 1) * LANES)
        k_ref[0, j] = k[:, sl].astype(BF16)
        _store_vt(vt_ref, j, v[:, sl])


def _kv_ctx_call(ctx, sh, sc, g, w_qkv):
    B, Lc, _ = ctx.shape
    assert Lc % LANES == 0
    tn = 512
    hps = tn // LANES
    n_steps = D // tn
    vec = pl.BlockSpec((1, D), lambda b, n: (0, 0))
    return pl.pallas_call(
        functools.partial(_kv_ctx_kernel, heads_per_step=hps),
        grid=(B, n_steps),
        in_specs=[
            pl.BlockSpec((1, Lc, D), lambda b, n: (b, 0, 0)),
            vec, vec, vec,
            pl.BlockSpec((D, tn), lambda b, n: (0, n + n_steps)),
            pl.BlockSpec((D, tn), lambda b, n: (0, n + 2 * n_steps)),
        ],
        out_specs=[pl.BlockSpec((1, hps, Lc, LANES), lambda b, n: (b, n, 0, 0)),
                   pl.BlockSpec((1, hps, VT_ROWS, Lc), lambda b, n: (b, n, 0, 0))],
        out_shape=[jax.ShapeDtypeStruct((B, N_HEADS, Lc, VALUE_DIM), BF16),
                   jax.ShapeDtypeStruct((B, N_HEADS, VT_ROWS, Lc), BF16)],
        scratch_shapes=[pltpu.VMEM((Lc, D), BF16)],
        compiler_params=_cparams(("parallel", "arbitrary")),
        name="kv_ctx",
    )(ctx, sh, sc, g, w_qkv, w_qkv)


def _sublane_fold(x, op):
    return op(x.reshape(x.shape[0] // SUBLANES, SUBLANES, x.shape[1]), axis=0)


def _attn_kernel(lam_ref, sg_ref, q_ref, kl_ref, vtl_ref, kc_ref, vtc_ref, *rest, lambda_init, chunks):
    n_cast = (len(rest) - 2) // 2
    o_ref, pv_scr = rest[n_cast], rest[-1]
    for w_ref, w_out in zip(rest[:n_cast], rest[n_cast + 1:-1]):
        w_out[...] = w_ref[...].astype(BF16)
    lf = lam_ref[...]
    lam = (jnp.exp(jnp.sum(lf[0:1] * lf[1:2], axis=-1, keepdims=True))
           - jnp.exp(jnp.sum(lf[2:3] * lf[3:4], axis=-1, keepdims=True)) + lambda_init)
    n_tiles = q_ref.shape[2] // Q_TILE
    lane = lax.broadcasted_iota(jnp.int32, (Q_TILE, VALUE_DIM), 1)
    zero = jnp.zeros((Q_TILE, VALUE_DIM), BF16)
    q_maps = []
    for ti in range(n_tiles):
        q = q_ref[0, 0, ti * Q_TILE:(ti + 1) * Q_TILE, :]
        q_maps.append((jnp.where(lane < HEAD_DIM, q, zero), jnp.where(lane >= HEAD_DIM, q, zero)))
    dn = (((1,), (1,)), ((), ()))
    tasks = [(ti, ci, mp) for ti in range(n_tiles) for ci in range(len(chunks)) for mp in range(2)]

    k_refs, vt_refs = (kl_ref, kc_ref), (vtl_ref, vtc_ref)

    def scores(t):
        ti, ci, mp = tasks[t]
        src, lo, hi = chunks[ci]
        return lax.dot_general(k_refs[src][0, 0, lo:hi, :], q_maps[ti][mp], dn,
                               preferred_element_type=F32)

    def merge(ti, maxes):
        ot = None
        for mp in range(2):
            m = functools.reduce(jnp.maximum, maxes[mp])
            w = [jnp.exp2(mc - m) for mc in maxes[mp]]
            den = functools.reduce(
                jnp.add, [wc * pv_scr[ti, mp, ci, VALUE_DIM:VALUE_DIM + 1, :] for ci, wc in enumerate(w)])
            norm = 1.0 / den if mp == 0 else -lam / den
            for ci, wc in enumerate(w):
                term = pv_scr[ti, mp, ci, :VALUE_DIM, :] * (wc * norm)
                ot = term if ot is None else ot + term
        ot = ot * lax.rsqrt(jnp.mean(ot * ot, axis=0, keepdims=True) + EPS)
        o_ref[0, ti * Q_TILE:(ti + 1) * Q_TILE, :] = (
            ot.T * (sg_ref[...] * (1.0 - lambda_init))).astype(BF16)

    pending = [scores(t) for t in range(SCORE_LOOKAHEAD)]
    maxes = ([], [])
    for t, (ti, ci, mp) in enumerate(tasks):
        if t + SCORE_LOOKAHEAD < len(tasks):
            pending.append(scores(t + SCORE_LOOKAHEAD))
        s = pending.pop(0)
        mc = jnp.max(_sublane_fold(s, jnp.max), axis=0, keepdims=True)
        e = jnp.exp2(s - mc).astype(BF16)
        maxes[mp].append(mc)
        src, lo, hi = chunks[ci]
        pv_scr[ti, mp, ci] = _mm(vt_refs[src][0, 0, :, lo:hi], e)
        if (ci, mp) == (len(chunks) - 1, 1):
            merge(ti, maxes)
            maxes = ([], [])


def _key_chunks(src, n_keys):
    assert n_keys % MXU_TILE == 0
    return tuple((src, lo, min(lo + KEY_CHUNK, n_keys)) for lo in range(0, n_keys, KEY_CHUNK))


def _cast_rows(n_rows, n_steps):
    rows = next(r for r in range(16, n_rows + 1, 16) if n_rows % r == 0 and r * n_steps >= n_rows)
    return rows, n_rows // rows


def _attn_call(lam, subln_g, q, k_lat, vt_lat, k_ctx, vt_ctx, lambda_init, weights):
    B, H, N, _ = q.shape
    n_ctx = k_ctx.shape[2]
    chunks = _key_chunks(0, N) + _key_chunks(1, n_ctx)
    tq = Q_TILE * Q_TILES_PER_STEP
    n_q = N // tq
    per_head = lambda b, h, i: (b, h, 0, 0)

    def cast_spec(w):
        rows, n_blocks = _cast_rows(w.shape[0], B * H * n_q)
        return pl.BlockSpec((rows, w.shape[1]),
                            lambda b, h, i: (jnp.minimum((b * H + h) * n_q + i, n_blocks - 1), 0))

    return pl.pallas_call(
        functools.partial(_attn_kernel, lambda_init=lambda_init, chunks=chunks),
        grid=(B, H, n_q),
        in_specs=[
            pl.BlockSpec((4, HEAD_DIM), lambda b, h, i: (0, 0)),
            pl.BlockSpec((1, VALUE_DIM), lambda b, h, i: (0, 0)),
            pl.BlockSpec((1, 1, tq, VALUE_DIM), lambda b, h, i: (b, h, i, 0)),
            pl.BlockSpec((1, 1, N, VALUE_DIM), per_head),
            pl.BlockSpec((1, 1, VT_ROWS, N), per_head),
            pl.BlockSpec((1, 1, n_ctx, VALUE_DIM), per_head),
            pl.BlockSpec((1, 1, VT_ROWS, n_ctx), per_head),
            *[cast_spec(w) for w in weights],
        ],
        out_specs=[pl.BlockSpec((1, tq, VALUE_DIM), lambda b, h, i: (b, i, h)),
                   *[cast_spec(w) for w in weights]],
        out_shape=[jax.ShapeDtypeStruct((B, N, D), BF16),
                   *[jax.ShapeDtypeStruct(w.shape, BF16) for w in weights]],
        scratch_shapes=[pltpu.VMEM((Q_TILES_PER_STEP, 2, len(chunks), VT_ROWS, Q_TILE), F32)],
        compiler_params=_cparams(("arbitrary", "arbitrary", "arbitrary")),
        name="attn",
    )(lam, subln_g.reshape(1, VALUE_DIM), q, k_lat, vt_lat, k_ctx, vt_ctx, *weights)


def _post_kernel(a_ref, w_ref, x_ref, gt_ref, g_ref, o_ref, *, pitched):
    if pitched:
        a = jnp.concatenate([a_ref[0, _slab(r), :] for r in range(a_ref.shape[1] // PITCH)], axis=0)
    else:
        a = a_ref[0]
    y = _mm(a.astype(BF16), w_ref[...])
    o_ref[0] = x_ref[0] + gt_ref[0] * _rms(y, g_ref[...])


def _post_call(a, w, x, gt, g, pitched=False):
    B, N, _ = x.shape
    tm = 512
    row = lambda b, m: (b, m, 0)
    return pl.pallas_call(
        functools.partial(_post_kernel, pitched=pitched),
        grid=(B, N // tm),
        in_specs=[
            pl.BlockSpec((1, tm // GRID_W * PITCH if pitched else tm, D), row),
            pl.BlockSpec((D, D), lambda b, m: (0, 0)),
            pl.BlockSpec((1, tm, D), row),
            pl.BlockSpec((1, 1, D), lambda b, m: (b, 0, 0)),
            pl.BlockSpec((1, D), lambda b, m: (0, 0)),
        ],
        out_specs=pl.BlockSpec((1, tm, D), row),
        out_shape=jax.ShapeDtypeStruct(x.shape, F32),
        compiler_params=_cparams(("parallel", "parallel")),
        name="post",
    )(a, w, x, gt, g)


def _ffn_kernel(x_ref, xn_ref, sh_ref, sc_ref, shn_ref, scn_ref, gt_ref, g_in_ref, g_out_ref,
                wg_ref, wu_ref, wd_ref, o_ref, h_scr, acc_scr, *, n_chunks):
    t, j = pl.program_id(0), pl.program_id(1)
    cur = lax.rem(t, 2)

    @pl.when((t == 0) & (j == 0))
    def _():
        h_scr[0] = _modulated_norm(x_ref[0], g_in_ref[...], sc_ref[0], sh_ref[0])

    @pl.when(j == 0)
    def _():
        acc_scr[...] = jnp.zeros_like(acc_scr)

    h = h_scr[cur]
    g = _mm(h, wg_ref[...])
    u = _mm(h, wu_ref[...])
    a = (g * jax.nn.sigmoid(g) * u).astype(BF16)
    acc_scr[...] += _mm(a, wd_ref[...])

    rows = _lookahead_rows(j, n_chunks, x_ref.shape[1])
    h_scr[1 - cur, rows, :] = _modulated_norm(xn_ref[0, rows, :], g_in_ref[...], scn_ref[0], shn_ref[0])

    @pl.when(j == n_chunks - 1)
    def _():
        o_ref[0] = x_ref[0] + gt_ref[0] * _rms(acc_scr[...], g_out_ref[...])


def _ffn_call(x, sh, sc, gt, g_in, g_out, w_gu, w_down, layer):
    B, N, _ = x.shape
    tm, th = 512, 512
    n_chunks = FFN // th
    tiles = N // tm
    n_tiles = B * tiles

    def nxt(t):
        return jnp.minimum(t + 1, n_tiles - 1)

    row = pl.BlockSpec((1, tm, D), lambda t, j: (t // tiles, t % tiles, 0))
    row_next = pl.BlockSpec((1, tm, D), lambda t, j: (nxt(t) // tiles, nxt(t) % tiles, 0))
    vec = pl.BlockSpec((1, 1, D), lambda t, j: (t // tiles, 0, 0))
    vec_next = pl.BlockSpec((1, 1, D), lambda t, j: (nxt(t) // tiles, 0, 0))
    gsp = pl.BlockSpec((1, D), lambda t, j: (0, 0))
    return pl.pallas_call(
        functools.partial(_ffn_kernel, n_chunks=n_chunks),
        grid=(n_tiles, n_chunks),
        in_specs=[
            row, row_next, vec, vec, vec_next, vec_next, vec, gsp, gsp,
            pl.BlockSpec((None, D, th), lambda t, j: (layer, 0, j)),
            pl.BlockSpec((None, D, th), lambda t, j: (layer, 0, j + n_chunks)),
            pl.BlockSpec((None, th, D), lambda t, j: (layer, j, 0)),
        ],
        out_specs=row,
        out_shape=jax.ShapeDtypeStruct(x.shape, F32),
        scratch_shapes=[pltpu.VMEM((2, tm, D), BF16), pltpu.VMEM((tm, D), F32)],
        compiler_params=_cparams(("arbitrary", "arbitrary")),
        name="ffn",
    )(x, x, sh, sc, sh, sc, gt, g_in, g_out, w_gu, w_gu, w_down)


PITCH = GRID_W + SUBLANES
N_PITCHED = GRID_W * PITCH


def _slab(i):
    return slice(i * PITCH, i * PITCH + GRID_W)


def _pad(i):
    return slice(i * PITCH + GRID_W, (i + 1) * PITCH)


def _complex_rows(re_ref, im_ref, col):
    rows = pl.ds(col, GRID_W, stride=PITCH)
    return jnp.concatenate([re_ref[0, rows, :].astype(BF16), im_ref[0, rows, :].astype(BF16)], axis=0)


def _fin_kernel(x_ref, xn_ref, sh_ref, sc_ref, shn_ref, scn_ref, g_ref, w_ref, u_ref, h_scr):
    t = pl.program_id(0)
    cur = lax.rem(t, 2)

    @pl.when(t == 0)
    def _():
        h_scr[0] = _modulated_norm(x_ref[0], g_ref[...], sc_ref[0], sh_ref[0])

    u = _mm(h_scr[cur], w_ref[...])
    for r in range(x_ref.shape[1] // GRID_W):
        u_ref[0, _slab(r), :] = u[r * GRID_W:(r + 1) * GRID_W]
        u_ref[0, _pad(r), :] = jnp.zeros((PITCH - GRID_W, D), F32)
    h_scr[1 - cur] = _modulated_norm(xn_ref[0], g_ref[...], scn_ref[0], shn_ref[0])


def _fin_call(x, sh, sc, g, w_in):
    B, N, _ = x.shape
    tm = 512
    tiles = N // tm
    nxt = _next_tile(B * tiles)
    row = pl.BlockSpec((1, tm, D), lambda t: (t // tiles, t % tiles, 0))
    row_next = pl.BlockSpec((1, tm, D), lambda t: (nxt(t) // tiles, nxt(t) % tiles, 0))
    vec = pl.BlockSpec((1, 1, D), lambda t: (t // tiles, 0, 0))
    vec_next = pl.BlockSpec((1, 1, D), lambda t: (nxt(t) // tiles, 0, 0))
    return pl.pallas_call(
        _fin_kernel,
        grid=(B * tiles,),
        in_specs=[
            row, row_next, vec, vec, vec_next, vec_next,
            pl.BlockSpec((1, D), lambda t: (0, 0)),
            pl.BlockSpec((D, D), lambda t: (0, 0)),
        ],
        out_specs=pl.BlockSpec((1, tm // GRID_W * PITCH, D), lambda t: (t // tiles, t % tiles, 0)),
        out_shape=jax.ShapeDtypeStruct((B, N_PITCHED, D), F32),
        scratch_shapes=[pltpu.VMEM((2, tm, D), BF16)],
        compiler_params=_cparams(("arbitrary",)),
        name="fin",
    )(x, x, sh, sc, sh, sc, g, w_in)


def _dft_a_kernel(ulo_ref, uhi_ref, cs_ref, m_ref, twc_ref, tws_ref, ar_ref, ai_ref):
    cs, m = cs_ref[...], m_ref[...]
    zero_pad = jnp.zeros((PITCH - GRID_W, GROUP_DIM), F32)
    batch = MXU_TILE // GRID_W
    for col0 in range(0, GRID_W, batch):
        u = []
        for col in range(col0, col0 + batch):
            rows = pl.ds(col, GRID_W, stride=PITCH)
            u.append(jnp.concatenate([ulo_ref[0, rows, :], uhi_ref[0, rows, :]], axis=1).astype(BF16))
        zb = _mm(jnp.concatenate(u, axis=0), cs)
        for i, col in enumerate(range(col0, col0 + batch)):
            z = zb[i * GRID_W:(i + 1) * GRID_W]
            z = jnp.concatenate([z[:, :GROUP_DIM], z[:, GROUP_DIM:]], axis=0).astype(BF16)
            a = _mm(m, z)
            ar, ai = a[:GRID_W], a[GRID_W:]
            c = jnp.concatenate([twc_ref[col]] * (GROUP_DIM // LANES), axis=1)
            s = jnp.concatenate([tws_ref[col]] * (GROUP_DIM // LANES), axis=1)
            ar_ref[0, _slab(col), :] = ar * c + ai * s
            ai_ref[0, _slab(col), :] = ai * c - ar * s
            ar_ref[0, _pad(col), :] = zero_pad
            ai_ref[0, _pad(col), :] = zero_pad


def _dft_a_call(u, cs, m_a, twc, tws):
    B = u.shape[0]
    halves = GROUP_DIM // LANES
    assert halves == 2
    lane_blk = lambda h: pl.BlockSpec((1, N_PITCHED, LANES), lambda b, i: (b, 0, halves * i + h))
    out = pl.BlockSpec((1, N_PITCHED, GROUP_DIM), lambda b, i: (b, 0, i))
    tw = pl.BlockSpec((GRID_W, GRID_W, LANES), lambda b, i: (0, 0, 0))
    shp = jax.ShapeDtypeStruct((B, N_PITCHED, D), F32)
    return pl.pallas_call(
        _dft_a_kernel,
        grid=(B, GROUPS),
        in_specs=[
            lane_blk(0), lane_blk(1),
            pl.BlockSpec((GROUP_DIM, 2 * GROUP_DIM), lambda b, i: (0, 0)),
            pl.BlockSpec((2 * GRID_W, 2 * GRID_W), lambda b, i: (0, 0)),
            tw, tw,
        ],
        out_specs=[out, out],
        out_shape=[shp, shp],
        compiler_params=_cparams(("parallel", "parallel")),
        name="dft_a",
    )(u, u, cs, m_a, twc, tws)


def _dft_c_kernel(ar_ref, ai_ref, m_ref, f_ref):
    m = m_ref[...]
    for kr in range(GRID_W):
        f = _mm(m, _complex_rows(ar_ref, ai_ref, kr))
        f_ref[0, pl.ds(kr, GRID_W, stride=PITCH), :] = f
    for j in range(GRID_W, PITCH):
        f_ref[0, pl.ds(j, GRID_W, stride=PITCH), :] = jnp.zeros((GRID_W, f_ref.shape[2]), F32)


def _dft_c_call(ar, ai, m_c):
    B = ar.shape[0]
    blk = pl.BlockSpec((1, N_PITCHED, LANES), lambda b, i: (b, 0, i))
    return pl.pallas_call(
        _dft_c_kernel,
        grid=(B, D // LANES),
        in_specs=[blk, blk, pl.BlockSpec((GRID_W, 2 * GRID_W), lambda b, i: (0, 0))],
        out_specs=blk,
        out_shape=jax.ShapeDtypeStruct((B, N_PITCHED, D), F32),
        compiler_params=_cparams(("parallel", "parallel")),
        name="dft_c",
    )(ar, ai, m_c)


def _rope_tables(n_tokens):
    pos = jnp.arange(n_tokens)
    lane = jnp.arange(LANES)
    d = lane % HEAD_DIM
    axis = d // (2 * ROPE_FREQS)
    second_half = (d % (2 * ROPE_FREQS)) >= ROPE_FREQS
    inv_freq = ROPE_THETA ** (-(d % ROPE_FREQS).astype(F32) / ROPE_FREQS)
    coord = jnp.where(axis[None, :] == 0, (pos // GRID_W)[:, None], (pos % GRID_W)[:, None])
    ang = coord.astype(F32) * inv_freq[None, :]
    c, s = jnp.cos(ang), jnp.sin(ang)
    return c, jnp.where(second_half[None, :], 0.0, -s), jnp.where(second_half[None, :], s, 0.0)


def _dft_tables():
    two_pi = 2.0 * math.pi
    ch = jnp.arange(GROUP_DIM)
    ang_c = two_pi * ((ch[:, None] * ch[None, :]) % GROUP_DIM).astype(F32) / GROUP_DIM
    ch_scale = GROUP_DIM ** -0.5
    cs = jnp.concatenate([jnp.cos(ang_c), -jnp.sin(ang_c)], axis=1) * ch_scale
    r = jnp.arange(GRID_W)
    ang_r = two_pi * ((r[:, None] * r[None, :]) % GRID_W).astype(F32) / GRID_W
    cr, sr = jnp.cos(ang_r) / 8.0, jnp.sin(ang_r) / 8.0
    m_a = jnp.concatenate([jnp.concatenate([cr, sr], axis=1),
                           jnp.concatenate([-sr, cr], axis=1)], axis=0)
    m_c = jnp.concatenate([cr, sr], axis=1)
    ang_t = two_pi * (r[:, None] * r[None, :]).astype(F32) / (GRID_W * GRID_W)
    twc = jnp.broadcast_to(jnp.cos(ang_t)[:, :, None], (GRID_W, GRID_W, LANES))
    tws = jnp.broadcast_to(jnp.sin(ang_t)[:, :, None], (GRID_W, GRID_W, LANES))
    return cs.astype(BF16), m_a.astype(BF16), m_c.astype(BF16), twc, tws


def kernel(x, c, ctx, c_ctx, mod_w, mod_b, norm_g, ffn_w_gu, ffn_w_down, attn_w_qkv, attn_w_o,
           attn_lambda, attn_subln_g, four_w_in, four_w_out):
    B, N, _ = x.shape
    Lc = ctx.shape[1]
    assert x.shape == (B, GRID_W * GRID_W, D) and ctx.shape == (B, Lc, D) and N % Lc == 0

    cond = jnp.concatenate([c, c_ctx[None, :], jnp.zeros((8 - B - 1, D), F32)], axis=0)
    mods = _ada_call(cond, mod_w, mod_b)

    def lat(layer, idx):
        return mods[layer, :B, idx * D:(idx + 1) * D].reshape(B, 1, D)

    def ctx_mod(layer, idx):
        return mods[layer, B:B + 1, idx * D:(idx + 1) * D]

    g = norm_g.reshape(norm_g.shape[0], 4, 1, D)

    lambda_init = 0.8 - 0.6 * math.exp(-0.3 * 0)
    rope_c, rope_slo, rope_shi = _rope_tables(N)
    w_qkv = attn_w_qkv[0].astype(BF16)
    q, k_lat, vt_lat = _qkv_call(x, lat(0, 0), lat(0, 1), g[0, 0], w_qkv, rope_c, rope_slo, rope_shi)
    k_ctx, vt_ctx = _kv_ctx_call(ctx, ctx_mod(0, 0), ctx_mod(0, 1), g[0, 0], w_qkv)
    later = [ffn_w_gu.reshape(-1, 2 * FFN), ffn_w_down.reshape(-1, D), attn_w_o[0], four_w_in[0], four_w_out[0]]
    o, w_gu, w_down, w_o, w_in, w_out = _attn_call(
        attn_lambda[0], attn_subln_g[0], q, k_lat, vt_lat, k_ctx, vt_ctx, lambda_init, later)
    w_gu, w_down = w_gu.reshape(ffn_w_gu.shape), w_down.reshape(ffn_w_down.shape)
    x = _post_call(o, w_o, x, lat(0, 2), g[0, 1])
    x = _ffn_call(x, lat(0, 3), lat(0, 4), lat(0, 5), g[0, 2], g[0, 3], w_gu, w_down, 0)

    cs, m_a, m_c, twc, tws = _dft_tables()
    u = _fin_call(x, lat(1, 0), lat(1, 1), g[1, 0], w_in)
    f = _dft_c_call(*_dft_a_call(u, cs, m_a, twc, tws), m_c)
    x = _post_call(f, w_out, x, lat(1, 2), g[1, 1], pitched=True)
    x = _ffn_call(x, lat(1, 3), lat(1, 4), lat(1, 5), g[1, 2], g[1, 3], w_gu, w_down, 1)
    return x
```

```python
import functools
import math

import jax
import jax.numpy as jnp
from jax import lax
from jax.experimental import pallas as pl
from jax.experimental.pallas import tpu as pltpu

D = 2048
HEAD_DIM = 64
VALUE_DIM = 2 * HEAD_DIM
N_HEADS = D // VALUE_DIM
GRID_W = 64
ROPE_FREQS = HEAD_DIM // 4
ROPE_THETA = 10000.0
GROUPS = 8
GROUP_DIM = D // GROUPS
FFN = ((8 * D // 3 + 255) // 256) * 256
EPS = 1e-6
LANES = 128
SUBLANES = 8
MXU_TILE = 256
KEY_CHUNK = MXU_TILE
VT_ROWS = VALUE_DIM + 16
Q_SCALE = HEAD_DIM ** -0.5 * math.log2(math.e)
Q_TILE = MXU_TILE
Q_TILES_PER_STEP = 8
SCORE_LOOKAHEAD = 8
VMEM_LIMIT = 56 * 1024 * 1024

F32 = jnp.float32
BF16 = jnp.bfloat16


def _cparams(sem):
    return pltpu.CompilerParams(dimension_semantics=sem, vmem_limit_bytes=VMEM_LIMIT)


def _rms(x, g):
    return x * lax.rsqrt(jnp.mean(x * x, axis=-1, keepdims=True) + EPS) * g


def _mm(a, b):
    return jnp.dot(a, b, preferred_element_type=F32)


def _modulated_norm(x, g, sc, sh):
    return (_rms(x, g) * (1.0 + sc) + sh).astype(BF16)


def _lookahead_rows(step, n_steps, tm):
    rows = -(-tm // n_steps)
    rows += -rows % 16
    start = jnp.minimum(step * rows, tm - rows)
    return pl.ds(pl.multiple_of(start, 16), rows)


def _next_tile(n_tiles):
    return lambda t: jnp.minimum(t + 1, n_tiles - 1)


def _ada_kernel(c_ref, w_ref, b_ref, o_ref):
    c = c_ref[...]
    s = (c * jax.nn.sigmoid(c)).astype(BF16)
    o_ref[0] = _mm(s, w_ref[0].astype(BF16)) + b_ref[0]


def _ada_call(cond, mod_w, mod_b):
    depth = mod_w.shape[0]
    tn = 1024
    return pl.pallas_call(
        _ada_kernel,
        grid=(depth, 6 * D // tn),
        in_specs=[
            pl.BlockSpec((8, D), lambda l, n: (0, 0)),
            pl.BlockSpec((1, D, tn), lambda l, n: (l, 0, n)),
            pl.BlockSpec((1, 1, tn), lambda l, n: (l, 0, n)),
        ],
        out_specs=pl.BlockSpec((1, 8, tn), lambda l, n: (l, 0, n)),
        out_shape=jax.ShapeDtypeStruct((depth, 8, 6 * D), F32),
        compiler_params=_cparams(("parallel", "parallel")),
        name="ada",
    )(cond, mod_w, mod_b.reshape(depth, 1, 6 * D))


def _rope(t, c, s_lo, s_hi):
    return t * c + pltpu.roll(t, 16, 1) * s_hi + pltpu.roll(t, LANES - 16, 1) * s_lo


def _store_vt(vt_ref, j, v):
    n = v.shape[0]
    row = lax.broadcasted_iota(jnp.int32, (VT_ROWS - VALUE_DIM, n), 0)
    extra = jnp.where(row == 0, 1.0, 0.0).astype(BF16)
    vt_ref[0, j] = jnp.concatenate([v.T.astype(BF16), extra], axis=0)


def _qkv_kernel(x_ref, xn_ref, sh_ref, sc_ref, shn_ref, scn_ref, g_ref, wq_ref, wk_ref, wv_ref,
                c_ref, slo_ref, shi_ref, q_ref, k_ref, vt_ref, h_scr, *, heads_per_step, n_steps):
    t, n = pl.program_id(0), pl.program_id(1)
    cur = lax.rem(t, 2)

    @pl.when((t == 0) & (n == 0))
    def _():
        h_scr[0] = _modulated_norm(x_ref[0], g_ref[...], sc_ref[0], sh_ref[0])

    h = h_scr[cur]
    q = _mm(h, wq_ref[...])
    k = _mm(h, wk_ref[...])
    v = _mm(h, wv_ref[...])
    c, s_lo, s_hi = c_ref[...], slo_ref[...], shi_ref[...]
    for j in range(heads_per_step):
        sl = slice(j * LANES, (j + 1) * LANES)
        q_ref[0, j] = (_rope(q[:, sl], c, s_lo, s_hi) * Q_SCALE).astype(BF16)
        k_ref[0, j] = _rope(k[:, sl], c, s_lo, s_hi).astype(BF16)
        _store_vt(vt_ref, j, v[:, sl])

    rows = _lookahead_rows(n, n_steps, x_ref.shape[1])
    h_scr[1 - cur, rows, :] = _modulated_norm(xn_ref[0, rows, :], g_ref[...], scn_ref[0], shn_ref[0])


def _qkv_call(x, sh, sc, g, w_qkv, rope_c, rope_slo, rope_shi):
    B, N, _ = x.shape
    tm, tn = 512, 512
    hps = tn // LANES
    n_steps = D // tn
    tiles = N // tm
    nxt = _next_tile(B * tiles)
    row = pl.BlockSpec((1, tm, D), lambda t, n: (t // tiles, t % tiles, 0))
    row_next = pl.BlockSpec((1, tm, D), lambda t, n: (nxt(t) // tiles, nxt(t) % tiles, 0))
    vec = pl.BlockSpec((1, 1, D), lambda t, n: (t // tiles, 0, 0))
    vec_next = pl.BlockSpec((1, 1, D), lambda t, n: (nxt(t) // tiles, 0, 0))
    tab = pl.BlockSpec((tm, LANES), lambda t, n: (t % tiles, 0))
    out = pl.BlockSpec((1, hps, tm, LANES), lambda t, n: (t // tiles, n, t % tiles, 0))
    return pl.pallas_call(
        functools.partial(_qkv_kernel, heads_per_step=hps, n_steps=n_steps),
        grid=(B * tiles, n_steps),
        in_specs=[
            row, row_next, vec, vec, vec_next, vec_next,
            pl.BlockSpec((1, D), lambda t, n: (0, 0)),
            pl.BlockSpec((D, tn), lambda t, n: (0, n)),
            pl.BlockSpec((D, tn), lambda t, n: (0, n + n_steps)),
            pl.BlockSpec((D, tn), lambda t, n: (0, n + 2 * n_steps)),
            tab, tab, tab,
        ],
        out_specs=[out, out,
                   pl.BlockSpec((1, hps, VT_ROWS, tm), lambda t, n: (t // tiles, n, 0, t % tiles))],
        out_shape=[jax.ShapeDtypeStruct((B, N_HEADS, N, VALUE_DIM), BF16),
                   jax.ShapeDtypeStruct((B, N_HEADS, N, VALUE_DIM), BF16),
                   jax.ShapeDtypeStruct((B, N_HEADS, VT_ROWS, N), BF16)],
        scratch_shapes=[pltpu.VMEM((2, tm, D), BF16)],
        compiler_params=_cparams(("arbitrary", "arbitrary")),
        name="qkv",
    )(x, x, sh, sc, sh, sc, g, w_qkv, w_qkv, w_qkv, rope_c, rope_slo, rope_shi)


def _kv_ctx_kernel(x_ref, sh_ref, sc_ref, g_ref, wk_ref, wv_ref, k_ref, vt_ref, h_scr, *, heads_per_step):
    @pl.when(pl.program_id(1) == 0)
    def _():
        h_scr[...] = _modulated_norm(x_ref[0], g_ref[...], sc_ref[...], sh_ref[...])

    h = h_scr[...]
    k = _mm(h, wk_ref[...])
    v = _mm(h, wv_ref[...])
    for j in range(heads_per_step):
        sl = slice(j * LANES, (j + 1) * LANES)
        k_ref[0, j] = k[:, sl].astype(BF16)
        _store_vt(vt_ref, j, v[:, sl])


def _kv_ctx_call(ctx, sh, sc, g, w_qkv):
    B, Lc, _ = ctx.shape
    assert Lc % LANES == 0
    tn = 512
    hps = tn // LANES
    n_steps = D // tn
    vec = pl.BlockSpec((1, D), lambda b, n: (0, 0))
    return pl.pallas_call(
        functools.partial(_kv_ctx_kernel, heads_per_step=hps),
        grid=(B, n_steps),
        in_specs=[
            pl.BlockSpec((1, Lc, D), lambda b, n: (b, 0, 0)),
            vec, vec, vec,
            pl.BlockSpec((D, tn), lambda b, n: (0, n + n_steps)),
            pl.BlockSpec((D, tn), lambda b, n: (0, n + 2 * n_steps)),
        ],
        out_specs=[pl.BlockSpec((1, hps, Lc, LANES), lambda b, n: (b, n, 0, 0)),
                   pl.BlockSpec((1, hps, VT_ROWS, Lc), lambda b, n: (b, n, 0, 0))],
        out_shape=[jax.ShapeDtypeStruct((B, N_HEADS, Lc, VALUE_DIM), BF16),
                   jax.ShapeDtypeStruct((B, N_HEADS, VT_ROWS, Lc), BF16)],
        scratch_shapes=[pltpu.VMEM((Lc, D), BF16)],
        compiler_params=_cparams(("parallel", "arbitrary")),
        name="kv_ctx",
    )(ctx, sh, sc, g, w_qkv, w_qkv)


def _sublane_fold(x, op):
    return op(x.reshape(x.shape[0] // SUBLANES, SUBLANES, x.shape[1]), axis=0)


def _attn_kernel(lam_ref, sg_ref, q_ref, kl_ref, vtl_ref, kc_ref, vtc_ref, *rest, lambda_init, chunks):
    n_cast = (len(rest) - 2) // 2
    o_ref, pv_scr = rest[n_cast], rest[-1]
    for w_ref, w_out in zip(rest[:n_cast], rest[n_cast + 1:-1]):
        w_out[...] = w_ref[...].astype(BF16)
    lf = lam_ref[...]
    lam = (jnp.exp(jnp.sum(lf[0:1] * lf[1:2], axis=-1, keepdims=True))
           - jnp.exp(jnp.sum(lf[2:3] * lf[3:4], axis=-1, keepdims=True)) + lambda_init)
    n_tiles = q_ref.shape[2] // Q_TILE
    lane = lax.broadcasted_iota(jnp.int32, (Q_TILE, VALUE_DIM), 1)
    zero = jnp.zeros((Q_TILE, VALUE_DIM), BF16)
    q_maps = []
    for ti in range(n_tiles):
        q = q_ref[0, 0, ti * Q_TILE:(ti + 1) * Q_TILE, :]
        q_maps.append((jnp.where(lane < HEAD_DIM, q, zero), jnp.where(lane >= HEAD_DIM, q, zero)))
    dn = (((1,), (1,)), ((), ()))
    tasks = [(ti, ci, mp) for ti in range(n_tiles) for ci in range(len(chunks)) for mp in range(2)]

    k_refs, vt_refs = (kl_ref, kc_ref), (vtl_ref, vtc_ref)

    def scores(t):
        ti, ci, mp = tasks[t]
        src, lo, hi = chunks[ci]
        return lax.dot_general(k_refs[src][0, 0, lo:hi, :], q_maps[ti][mp], dn,
                               preferred_element_type=F32)

    def merge(ti, maxes):
        ot = None
        for mp in range(2):
            m = functools.reduce(jnp.maximum, maxes[mp])
            w = [jnp.exp2(mc - m) for mc in maxes[mp]]
            den = functools.reduce(
                jnp.add, [wc * pv_scr[ti, mp, ci, VALUE_DIM:VALUE_DIM + 1, :] for ci, wc in enumerate(w)])
            norm = 1.0 / den if mp == 0 else -lam / den
            for ci, wc in enumerate(w):
                term = pv_scr[ti, mp, ci, :VALUE_DIM, :] * (wc * norm)
                ot = term if ot is None else ot + term
        ot = ot * lax.rsqrt(jnp.mean(ot * ot, axis=0, keepdims=True) + EPS)
        o_ref[0, ti * Q_TILE:(ti + 1) * Q_TILE, :] = (
            ot.T * (sg_ref[...] * (1.0 - lambda_init))).astype(BF16)

    pending = [scores(t) for t in range(SCORE_LOOKAHEAD)]
    maxes = ([], [])
    for t, (ti, ci, mp) in enumerate(tasks):
        if t + SCORE_LOOKAHEAD < len(tasks):
            pending.append(scores(t + SCORE_LOOKAHEAD))
        s = pending.pop(0)
        mc = jnp.max(_sublane_fold(s, jnp.max), axis=0, keepdims=True)
        e = jnp.exp2(s - mc).astype(BF16)
        maxes[mp].append(mc)
        src, lo, hi = chunks[ci]
        pv_scr[ti, mp, ci] = _mm(vt_refs[src][0, 0, :, lo:hi], e)
        if (ci, mp) == (len(chunks) - 1, 1):
            merge(ti, maxes)
            maxes = ([], [])


def _key_chunks(src, n_keys):
    assert n_keys % MXU_TILE == 0
    return tuple((src, lo, min(lo + KEY_CHUNK, n_keys)) for lo in range(0, n_keys, KEY_CHUNK))


def _cast_rows(n_rows, n_steps):
    rows = next(r for r in range(16, n_rows + 1, 16) if n_rows % r == 0 and r * n_steps >= n_rows)
    return rows, n_rows // rows


def _attn_call(lam, subln_g, q, k_lat, vt_lat, k_ctx, vt_ctx, lambda_init, weights):
    B, H, N, _ = q.shape
    n_ctx = k_ctx.shape[2]
    chunks = _key_chunks(0, N) + _key_chunks(1, n_ctx)
    tq = Q_TILE * Q_TILES_PER_STEP
    n_q = N // tq
    per_head = lambda b, h, i: (b, h, 0, 0)

    def cast_spec(w):
        rows, n_blocks = _cast_rows(w.shape[0], B * H * n_q)
        return pl.BlockSpec((rows, w.shape[1]),
                            lambda b, h, i: (jnp.minimum((b * H + h) * n_q + i, n_blocks - 1), 0))

    return pl.pallas_call(
        functools.partial(_attn_kernel, lambda_init=lambda_init, chunks=chunks),
        grid=(B, H, n_q),
        in_specs=[
            pl.BlockSpec((4, HEAD_DIM), lambda b, h, i: (0, 0)),
            pl.BlockSpec((1, VALUE_DIM), lambda b, h, i: (0, 0)),
            pl.BlockSpec((1, 1, tq, VALUE_DIM), lambda b, h, i: (b, h, i, 0)),
            pl.BlockSpec((1, 1, N, VALUE_DIM), per_head),
            pl.BlockSpec((1, 1, VT_ROWS, N), per_head),
            pl.BlockSpec((1, 1, n_ctx, VALUE_DIM), per_head),
            pl.BlockSpec((1, 1, VT_ROWS, n_ctx), per_head),
            *[cast_spec(w) for w in weights],
        ],
        out_specs=[pl.BlockSpec((1, tq, VALUE_DIM), lambda b, h, i: (b, i, h)),
                   *[cast_spec(w) for w in weights]],
        out_shape=[jax.ShapeDtypeStruct((B, N, D), BF16),
                   *[jax.ShapeDtypeStruct(w.shape, BF16) for w in weights]],
        scratch_shapes=[pltpu.VMEM((Q_TILES_PER_STEP, 2, len(chunks), VT_ROWS, Q_TILE), F32)],
        compiler_params=_cparams(("arbitrary", "arbitrary", "arbitrary")),
        name="attn",
    )(lam, subln_g.reshape(1, VALUE_DIM), q, k_lat, vt_lat, k_ctx, vt_ctx, *weights)


def _post_kernel(a_ref, w_ref, x_ref, gt_ref, g_ref, o_ref, *, pitched):
    if pitched:
        a = jnp.concatenate([a_ref[0, _slab(r), :] for r in range(a_ref.shape[1] // PITCH)], axis=0)
    else:
        a = a_ref[0]
    y = _mm(a.astype(BF16), w_ref[...])
    o_ref[0] = x_ref[0] + gt_ref[0] * _rms(y, g_ref[...])


def _post_call(a, w, x, gt, g, pitched=False):
    B, N, _ = x.shape
    tm = 512
    row = lambda b, m: (b, m, 0)
    return pl.pallas_call(
        functools.partial(_post_kernel, pitched=pitched),
        grid=(B, N // tm),
        in_specs=[
            pl.BlockSpec((1, tm // GRID_W * PITCH if pitched else tm, D), row),
            pl.BlockSpec((D, D), lambda b, m: (0, 0)),
            pl.BlockSpec((1, tm, D), row),
            pl.BlockSpec((1, 1, D), lambda b, m: (b, 0, 0)),
            pl.BlockSpec((1, D), lambda b, m: (0, 0)),
        ],
        out_specs=pl.BlockSpec((1, tm, D), row),
        out_shape=jax.ShapeDtypeStruct(x.shape, F32),
        compiler_params=_cparams(("parallel", "parallel")),
        name="post",
    )(a, w, x, gt, g)


def _ffn_kernel(x_ref, xn_ref, sh_ref, sc_ref, shn_ref, scn_ref, gt_ref, g_in_ref, g_out_ref,
                wg_ref, wu_ref, wd_ref, o_ref, h_scr, acc_scr, *, n_chunks):
    t, j = pl.program_id(0), pl.program_id(1)
    cur = lax.rem(t, 2)

    @pl.when((t == 0) & (j == 0))
    def _():
        h_scr[0] = _modulated_norm(x_ref[0], g_in_ref[...], sc_ref[0], sh_ref[0])

    @pl.when(j == 0)
    def _():
        acc_scr[...] = jnp.zeros_like(acc_scr)

    h = h_scr[cur]
    g = _mm(h, wg_ref[...])
    u = _mm(h, wu_ref[...])
    a = (g * jax.nn.sigmoid(g) * u).astype(BF16)
    acc_scr[...] += _mm(a, wd_ref[...])

    rows = _lookahead_rows(j, n_chunks, x_ref.shape[1])
    h_scr[1 - cur, rows, :] = _modulated_norm(xn_ref[0, rows, :], g_in_ref[...], scn_ref[0], shn_ref[0])

    @pl.when(j == n_chunks - 1)
    def _():
        o_ref[0] = x_ref[0] + gt_ref[0] * _rms(acc_scr[...], g_out_ref[...])


def _ffn_call(x, sh, sc, gt, g_in, g_out, w_gu, w_down, layer):
    B, N, _ = x.shape
    tm, th = 512, 512
    n_chunks = FFN // th
    tiles = N // tm
    n_tiles = B * tiles

    def nxt(t):
        return jnp.minimum(t + 1, n_tiles - 1)

    row = pl.BlockSpec((1, tm, D), lambda t, j: (t // tiles, t % tiles, 0))
    row_next = pl.BlockSpec((1, tm, D), lambda t, j: (nxt(t) // tiles, nxt(t) % tiles, 0))
    vec = pl.BlockSpec((1, 1, D), lambda t, j: (t // tiles, 0, 0))
    vec_next = pl.BlockSpec((1, 1, D), lambda t, j: (nxt(t) // tiles, 0, 0))
    gsp = pl.BlockSpec((1, D), lambda t, j: (0, 0))
    return pl.pallas_call(
        functools.partial(_ffn_kernel, n_chunks=n_chunks),
        grid=(n_tiles, n_chunks),
        in_specs=[
            row, row_next, vec, vec, vec_next, vec_next, vec, gsp, gsp,
            pl.BlockSpec((None, D, th), lambda t, j: (layer, 0, j)),
            pl.BlockSpec((None, D, th), lambda t, j: (layer, 0, j + n_chunks)),
            pl.BlockSpec((None, th, D), lambda t, j: (layer, j, 0)),
        ],
        out_specs=row,
        out_shape=jax.ShapeDtypeStruct(x.shape, F32),
        scratch_shapes=[pltpu.VMEM((2, tm, D), BF16), pltpu.VMEM((tm, D), F32)],
        compiler_params=_cparams(("arbitrary", "arbitrary")),
        name="ffn",
    )(x, x, sh, sc, sh, sc, gt, g_in, g_out, w_gu, w_gu, w_down)


PITCH = GRID_W + SUBLANES
N_PITCHED = GRID_W * PITCH


def _slab(i):
    return slice(i * PITCH, i * PITCH + GRID_W)


def _pad(i):
    return slice(i * PITCH + GRID_W, (i + 1) * PITCH)


def _complex_rows(re_ref, im_ref, col):
    rows = pl.ds(col, GRID_W, stride=PITCH)
    return jnp.concatenate([re_ref[0, rows, :].astype(BF16), im_ref[0, rows, :].astype(BF16)], axis=0)


def _fin_kernel(x_ref, xn_ref, sh_ref, sc_ref, shn_ref, scn_ref, g_ref, w_ref, u_ref, h_scr):
    t = pl.program_id(0)
    cur = lax.rem(t, 2)

    @pl.when(t == 0)
    def _():
        h_scr[0] = _modulated_norm(x_ref[0], g_ref[...], sc_ref[0], sh_ref[0])

    u = _mm(h_scr[cur], w_ref[...])
    for r in range(x_ref.shape[1] // GRID_W):
        u_ref[0, _slab(r), :] = u[r * GRID_W:(r + 1) * GRID_W]
        u_ref[0, _pad(r), :] = jnp.zeros((PITCH - GRID_W, D), F32)
    h_scr[1 - cur] = _modulated_norm(xn_ref[0], g_ref[...], scn_ref[0], shn_ref[0])


def _fin_call(x, sh, sc, g, w_in):
    B, N, _ = x.shape
    tm = 512
    tiles = N // tm
    nxt = _next_tile(B * tiles)
    row = pl.BlockSpec((1, tm, D), lambda t: (t // tiles, t % tiles, 0))
    row_next = pl.BlockSpec((1, tm, D), lambda t: (nxt(t) // tiles, nxt(t) % tiles, 0))
    vec = pl.BlockSpec((1, 1, D), lambda t: (t // tiles, 0, 0))
    vec_next = pl.BlockSpec((1, 1, D), lambda t: (nxt(t) // tiles, 0, 0))
    return pl.pallas_call(
        _fin_kernel,
        grid=(B * tiles,),
        in_specs=[
            row, row_next, vec, vec, vec_next, vec_next,
            pl.BlockSpec((1, D), lambda t: (0, 0)),
            pl.BlockSpec((D, D), lambda t: (0, 0)),
        ],
        out_specs=pl.BlockSpec((1, tm // GRID_W * PITCH, D), lambda t: (t // tiles, t % tiles, 0)),
        out_shape=jax.ShapeDtypeStruct((B, N_PITCHED, D), F32),
        scratch_shapes=[pltpu.VMEM((2, tm, D), BF16)],
        compiler_params=_cparams(("arbitrary",)),
        name="fin",
    )(x, x, sh, sc, sh, sc, g, w_in)


def _dft_a_kernel(ulo_ref, uhi_ref, cs_ref, m_ref, twc_ref, tws_ref, ar_ref, ai_ref):
    cs, m = cs_ref[...], m_ref[...]
    zero_pad = jnp.zeros((PITCH - GRID_W, GROUP_DIM), F32)
    batch = MXU_TILE // GRID_W
    for col0 in range(0, GRID_W, batch):
        u = []
        for col in range(col0, col0 + batch):
            rows = pl.ds(col, GRID_W, stride=PITCH)
            u.append(jnp.concatenate([ulo_ref[0, rows, :], uhi_ref[0, rows, :]], axis=1).astype(BF16))
        zb = _mm(jnp.concatenate(u, axis=0), cs)
        for i, col in enumerate(range(col0, col0 + batch)):
            z = zb[i * GRID_W:(i + 1) * GRID_W]
            z = jnp.concatenate([z[:, :GROUP_DIM], z[:, GROUP_DIM:]], axis=0).astype(BF16)
            a = _mm(m, z)
            ar, ai = a[:GRID_W], a[GRID_W:]
            c = jnp.concatenate([twc_ref[col]] * (GROUP_DIM // LANES), axis=1)
            s = jnp.concatenate([tws_ref[col]] * (GROUP_DIM // LANES), axis=1)
            ar_ref[0, _slab(col), :] = ar * c + ai * s
            ai_ref[0, _slab(col), :] = ai * c - ar * s
            ar_ref[0, _pad(col), :] = zero_pad
            ai_ref[0, _pad(col), :] = zero_pad


def _dft_a_call(u, cs, m_a, twc, tws):
    B = u.shape[0]
    halves = GROUP_DIM // LANES
    assert halves == 2
    lane_blk = lambda h: pl.BlockSpec((1, N_PITCHED, LANES), lambda b, i: (b, 0, halves * i + h))
    out = pl.BlockSpec((1, N_PITCHED, GROUP_DIM), lambda b, i: (b, 0, i))
    tw = pl.BlockSpec((GRID_W, GRID_W, LANES), lambda b, i: (0, 0, 0))
    shp = jax.ShapeDtypeStruct((B, N_PITCHED, D), F32)
    return pl.pallas_call(
        _dft_a_kernel,
        grid=(B, GROUPS),
        in_specs=[
            lane_blk(0), lane_blk(1),
            pl.BlockSpec((GROUP_DIM, 2 * GROUP_DIM), lambda b, i: (0, 0)),
            pl.BlockSpec((2 * GRID_W, 2 * GRID_W), lambda b, i: (0, 0)),
            tw, tw,
        ],
        out_specs=[out, out],
        out_shape=[shp, shp],
        compiler_params=_cparams(("parallel", "parallel")),
        name="dft_a",
    )(u, u, cs, m_a, twc, tws)


def _dft_c_kernel(ar_ref, ai_ref, m_ref, f_ref):
    m = m_ref[...]
    for kr in range(GRID_W):
        f = _mm(m, _complex_rows(ar_ref, ai_ref, kr))
        f_ref[0, pl.ds(kr, GRID_W, stride=PITCH), :] = f
    for j in range(GRID_W, PITCH):
        f_ref[0, pl.ds(j, GRID_W, stride=PITCH), :] = jnp.zeros((GRID_W, f_ref.shape[2]), F32)


def _dft_c_call(ar, ai, m_c):
    B = ar.shape[0]
    blk = pl.BlockSpec((1, N_PITCHED, LANES), lambda b, i: (b, 0, i))
    return pl.pallas_call(
        _dft_c_kernel,
        grid=(B, D // LANES),
        in_specs=[blk, blk, pl.BlockSpec((GRID_W, 2 * GRID_W), lambda b, i: (0, 0))],
        out_specs=blk,
        out_shape=jax.ShapeDtypeStruct((B, N_PITCHED, D), F32),
        compiler_params=_cparams(("parallel", "parallel")),
        name="dft_c",
    )(ar, ai, m_c)


def _rope_tables(n_tokens):
    pos = jnp.arange(n_tokens)
    lane = jnp.arange(LANES)
    d = lane % HEAD_DIM
    axis = d // (2 * ROPE_FREQS)
    second_half = (d % (2 * ROPE_FREQS)) >= ROPE_FREQS
    inv_freq = ROPE_THETA ** (-(d % ROPE_FREQS).astype(F32) / ROPE_FREQS)
    coord = jnp.where(axis[None, :] == 0, (pos // GRID_W)[:, None], (pos % GRID_W)[:, None])
    ang = coord.astype(F32) * inv_freq[None, :]
    c, s = jnp.cos(ang), jnp.sin(ang)
    return c, jnp.where(second_half[None, :], 0.0, -s), jnp.where(second_half[None, :], s, 0.0)


def _dft_tables():
    two_pi = 2.0 * math.pi
    ch = jnp.arange(GROUP_DIM)
    ang_c = two_pi * ((ch[:, None] * ch[None, :]) % GROUP_DIM).astype(F32) / GROUP_DIM
    ch_scale = GROUP_DIM ** -0.5
    cs = jnp.concatenate([jnp.cos(ang_c), -jnp.sin(ang_c)], axis=1) * ch_scale
    r = jnp.arange(GRID_W)
    ang_r = two_pi * ((r[:, None] * r[None, :]) % GRID_W).astype(F32) / GRID_W
    cr, sr = jnp.cos(ang_r) / 8.0, jnp.sin(ang_r) / 8.0
    m_a = jnp.concatenate([jnp.concatenate([cr, sr], axis=1),
                           jnp.concatenate([-sr, cr], axis=1)], axis=0)
    m_c = jnp.concatenate([cr, sr], axis=1)
    ang_t = two_pi * (r[:, None] * r[None, :]).astype(F32) / (GRID_W * GRID_W)
    twc = jnp.broadcast_to(jnp.cos(ang_t)[:, :, None], (GRID_W, GRID_W, LANES))
    tws = jnp.broadcast_to(jnp.sin(ang_t)[:, :, None], (GRID_W, GRID_W, LANES))
    return cs.astype(BF16), m_a.astype(BF16), m_c.astype(BF16), twc, tws


def kernel(x, c, ctx, c_ctx, mod_w, mod_b, norm_g, ffn_w_gu, ffn_w_down, attn_w_qkv, attn_w_o,
           attn_lambda, attn_subln_g, four_w_in, four_w_out):
    B, N, _ = x.shape
    Lc = ctx.shape[1]
    assert x.shape == (B, GRID_W * GRID_W, D) and ctx.shape == (B, Lc, D) and N % Lc == 0

    cond = jnp.concatenate([c, c_ctx[None, :], jnp.zeros((8 - B - 1, D), F32)], axis=0)
    mods = _ada_call(cond, mod_w, mod_b)

    def lat(layer, idx):
        return mods[layer, :B, idx * D:(idx + 1) * D].reshape(B, 1, D)

    def ctx_mod(layer, idx):
        return mods[layer, B:B + 1, idx * D:(idx + 1) * D]

    g = norm_g.reshape(norm_g.shape[0], 4, 1, D)

    lambda_init = 0.8 - 0.6 * math.exp(-0.3 * 0)
    rope_c, rope_slo, rope_shi = _rope_tables(N)
    w_qkv = attn_w_qkv[0].astype(BF16)
    q, k_lat, vt_lat = _qkv_call(x, lat(0, 0), lat(0, 1), g[0, 0], w_qkv, rope_c, rope_slo, rope_shi)
    k_ctx, vt_ctx = _kv_ctx_call(ctx, ctx_mod(0, 0), ctx_mod(0, 1), g[0, 0], w_qkv)
    later = [ffn_w_gu.reshape(-1, 2 * FFN), ffn_w_down.reshape(-1, D), attn_w_o[0], four_w_in[0], four_w_out[0]]
    o, w_gu, w_down, w_o, w_in, w_out = _attn_call(
        attn_lambda[0], attn_subln_g[0], q, k_lat, vt_lat, k_ctx, vt_ctx, lambda_init, later)
    w_gu, w_down = w_gu.reshape(ffn_w_gu.shape), w_down.reshape(ffn_w_down.shape)
    x = _post_call(o, w_o, x, lat(0, 2), g[0, 1])
    x = _ffn_call(x, lat(0, 3), lat(0, 4), lat(0, 5), g[0, 2], g[0, 3], w_gu, w_down, 0)

    cs, m_a, m_c, twc, tws = _dft_tables()
    u = _fin_call(x, lat(1, 0), lat(1, 1), g[1, 0], w_in)
    f = _dft_c_call(*_dft_a_call(u, cs, m_a, twc, tws), m_c)
    x = _post_call(f, w_out, x, lat(1, 2), g[1, 1], pitched=True)
    x = _ffn_call(x, lat(1, 3), lat(1, 4), lat(1, 5), g[1, 2], g[1, 3], w_gu, w_down, 1)
    return x
```

```python
import functools
import math

import jax
import jax.numpy as jnp
from jax import lax
from jax.experimental import pallas as pl
from jax.experimental.pallas import tpu as pltpu

D = 2048
HEAD_DIM = 64
VALUE_DIM = 2 * HEAD_DIM
N_HEADS = D // VALUE_DIM
GRID_W = 64
ROPE_FREQS = HEAD_DIM // 4
ROPE_THETA = 10000.0
GROUPS = 8
GROUP_DIM = D // GROUPS
FFN = ((8 * D // 3 + 255) // 256) * 256
EPS = 1e-6
LANES = 128
SUBLANES = 8
MXU_TILE = 256
KEY_CHUNK = MXU_TILE
VT_ROWS = VALUE_DIM + 16
Q_SCALE = HEAD_DIM ** -0.5 * math.log2(math.e)
Q_TILE = MXU_TILE
Q_TILES_PER_STEP = 8
SCORE_LOOKAHEAD = 8
VMEM_LIMIT = 56 * 1024 * 1024

F32 = jnp.float32
BF16 = jnp.bfloat16


def _cparams(sem):
    return pltpu.CompilerParams(dimension_semantics=sem, vmem_limit_bytes=VMEM_LIMIT)


def _rms(x, g):
    return x * lax.rsqrt(jnp.mean(x * x, axis=-1, keepdims=True) + EPS) * g


def _mm(a, b):
    return jnp.dot(a, b, preferred_element_type=F32)


def _modulated_norm(x, g, sc, sh):
    return (_rms(x, g) * (1.0 + sc) + sh).astype(BF16)


def _lookahead_rows(step, n_steps, tm):
    rows = -(-tm // n_steps)
    rows += -rows % 16
    start = jnp.minimum(step * rows, tm - rows)
    return pl.ds(pl.multiple_of(start, 16), rows)


def _next_tile(n_tiles):
    return lambda t: jnp.minimum(t + 1, n_tiles - 1)


def _ada_kernel(c_ref, w_ref, b_ref, o_ref):
    c = c_ref[...]
    s = (c * jax.nn.sigmoid(c)).astype(BF16)
    o_ref[0] = _mm(s, w_ref[0].astype(BF16)) + b_ref[0]


def _ada_call(cond, mod_w, mod_b):
    depth = mod_w.shape[0]
    tn = 1024
    return pl.pallas_call(
        _ada_kernel,
        grid=(depth, 6 * D // tn),
        in_specs=[
            pl.BlockSpec((8, D), lambda l, n: (0, 0)),
            pl.BlockSpec((1, D, tn), lambda l, n: (l, 0, n)),
            pl.BlockSpec((1, 1, tn), lambda l, n: (l, 0, n)),
        ],
        out_specs=pl.BlockSpec((1, 8, tn), lambda l, n: (l, 0, n)),
        out_shape=jax.ShapeDtypeStruct((depth, 8, 6 * D), F32),
        compiler_params=_cparams(("parallel", "parallel")),
        name="ada",
    )(cond, mod_w, mod_b.reshape(depth, 1, 6 * D))


def _rope(t, c, s_lo, s_hi):
    return t * c + pltpu.roll(t, 16, 1) * s_hi + pltpu.roll(t, LANES - 16, 1) * s_lo


def _store_vt(vt_ref, j, v):
    n = v.shape[0]
    row = lax.broadcasted_iota(jnp.int32, (VT_ROWS - VALUE_DIM, n), 0)
    extra = jnp.where(row == 0, 1.0, 0.0).astype(BF16)
    vt_ref[0, j] = jnp.concatenate([v.T.astype(BF16), extra], axis=0)


def _qkv_kernel(x_ref, sh_ref, sc_ref, g_ref, wq_ref, wk_ref, wv_ref,
                c_ref, slo_ref, shi_ref, q_ref, k_ref, vt_ref, h_scr, *, heads_per_step):
    @pl.when(pl.program_id(1) == 0)
    def _():
        h_scr[...] = _modulated_norm(x_ref[0], g_ref[...], sc_ref[0], sh_ref[0])

    h = h_scr[...]
    q = _mm(h, wq_ref[...])
    k = _mm(h, wk_ref[...])
    v = _mm(h, wv_ref[...])
    c, s_lo, s_hi = c_ref[...], slo_ref[...], shi_ref[...]
    for j in range(heads_per_step):
        sl = slice(j * LANES, (j + 1) * LANES)
        q_ref[0, j] = (_rope(q[:, sl], c, s_lo, s_hi) * Q_SCALE).astype(BF16)
        k_ref[0, j] = _rope(k[:, sl], c, s_lo, s_hi).astype(BF16)
        _store_vt(vt_ref, j, v[:, sl])


def _qkv_call(x, sh, sc, g, w_qkv, rope_c, rope_slo, rope_shi):
    B, N, _ = x.shape
    tm, tn = 512, 512
    hps = tn // LANES
    n_steps = D // tn
    tiles = N // tm
    row = pl.BlockSpec((1, tm, D), lambda t, n: (t // tiles, t % tiles, 0))
    vec = pl.BlockSpec((1, 1, D), lambda t, n: (t // tiles, 0, 0))
    tab = pl.BlockSpec((tm, LANES), lambda t, n: (t % tiles, 0))
    out = pl.BlockSpec((1, hps, tm, LANES), lambda t, n: (t // tiles, n, t % tiles, 0))
    return pl.pallas_call(
        functools.partial(_qkv_kernel, heads_per_step=hps),
        grid=(B * tiles, n_steps),
        in_specs=[
            row, vec, vec,
            pl.BlockSpec((1, D), lambda t, n: (0, 0)),
            pl.BlockSpec((D, tn), lambda t, n: (0, n)),
            pl.BlockSpec((D, tn), lambda t, n: (0, n + n_steps)),
            pl.BlockSpec((D, tn), lambda t, n: (0, n + 2 * n_steps)),
            tab, tab, tab,
        ],
        out_specs=[out, out,
                   pl.BlockSpec((1, hps, VT_ROWS, tm), lambda t, n: (t // tiles, n, 0, t % tiles))],
        out_shape=[jax.ShapeDtypeStruct((B, N_HEADS, N, VALUE_DIM), BF16),
                   jax.ShapeDtypeStruct((B, N_HEADS, N, VALUE_DIM), BF16),
                   jax.ShapeDtypeStruct((B, N_HEADS, VT_ROWS, N), BF16)],
        scratch_shapes=[pltpu.VMEM((tm, D), BF16)],
        compiler_params=_cparams(("parallel", "arbitrary")),
        name="qkv",
    )(x, sh, sc, g, w_qkv, w_qkv, w_qkv, rope_c, rope_slo, rope_shi)


def _kv_ctx_kernel(x_ref, sh_ref, sc_ref, g_ref, wk_ref, wv_ref, k_ref, vt_ref, h_scr, *, heads_per_step):
    @pl.when(pl.program_id(1) == 0)
    def _():
        h_scr[...] = _modulated_norm(x_ref[0], g_ref[...], sc_ref[...], sh_ref[...])

    h = h_scr[...]
    k = _mm(h, wk_ref[...])
    v = _mm(h, wv_ref[...])
    for j in range(heads_per_step):
        sl = slice(j * LANES, (j + 1) * LANES)
        k_ref[0, j] = k[:, sl].astype(BF16)
        _store_vt(vt_ref, j, v[:, sl])


def _kv_ctx_call(ctx, sh, sc, g, w_qkv):
    B, Lc, _ = ctx.shape
    assert Lc % LANES == 0
    tn = 512
    hps = tn // LANES
    n_steps = D // tn
    vec = pl.BlockSpec((1, D), lambda b, n: (0, 0))
    return pl.pallas_call(
        functools.partial(_kv_ctx_kernel, heads_per_step=hps),
        grid=(B, n_steps),
        in_specs=[
            pl.BlockSpec((1, Lc, D), lambda b, n: (b, 0, 0)),
            vec, vec, vec,
            pl.BlockSpec((D, tn), lambda b, n: (0, n + n_steps)),
            pl.BlockSpec((D, tn), lambda b, n: (0, n + 2 * n_steps)),
        ],
        out_specs=[pl.BlockSpec((1, hps, Lc, LANES), lambda b, n: (b, n, 0, 0)),
                   pl.BlockSpec((1, hps, VT_ROWS, Lc), lambda b, n: (b, n, 0, 0))],
        out_shape=[jax.ShapeDtypeStruct((B, N_HEADS, Lc, VALUE_DIM), BF16),
                   jax.ShapeDtypeStruct((B, N_HEADS, VT_ROWS, Lc), BF16)],
        scratch_shapes=[pltpu.VMEM((Lc, D), BF16)],
        compiler_params=_cparams(("parallel", "arbitrary")),
        name="kv_ctx",
    )(ctx, sh, sc, g, w_qkv, w_qkv)


def _sublane_fold(x, op):
    return op(x.reshape(x.shape[0] // SUBLANES, SUBLANES, x.shape[1]), axis=0)


def _attn_kernel(lam_ref, sg_ref, q_ref, kl_ref, vtl_ref, kc_ref, vtc_ref, *rest, lambda_init, chunks):
    n_cast = (len(rest) - 2) // 2
    o_ref, pv_scr = rest[n_cast], rest[-1]
    for w_ref, w_out in zip(rest[:n_cast], rest[n_cast + 1:-1]):
        w_out[...] = w_ref[...].astype(BF16)
    lf = lam_ref[...]
    lam = (jnp.exp(jnp.sum(lf[0:1] * lf[1:2], axis=-1, keepdims=True))
           - jnp.exp(jnp.sum(lf[2:3] * lf[3:4], axis=-1, keepdims=True)) + lambda_init)
    n_tiles = q_ref.shape[2] // Q_TILE
    lane = lax.broadcasted_iota(jnp.int32, (Q_TILE, VALUE_DIM), 1)
    zero = jnp.zeros((Q_TILE, VALUE_DIM), BF16)
    q_maps = []
    for ti in range(n_tiles):
        q = q_ref[0, 0, ti * Q_TILE:(ti + 1) * Q_TILE, :]
        q_maps.append((jnp.where(lane < HEAD_DIM, q, zero), jnp.where(lane >= HEAD_DIM, q, zero)))
    dn = (((1,), (1,)), ((), ()))
    tasks = [(ti, ci, mp) for ti in range(n_tiles) for ci in range(len(chunks)) for mp in range(2)]

    k_refs, vt_refs = (kl_ref, kc_ref), (vtl_ref, vtc_ref)

    def scores(t):
        ti, ci, mp = tasks[t]
        src, lo, hi = chunks[ci]
        return lax.dot_general(k_refs[src][0, 0, lo:hi, :], q_maps[ti][mp], dn,
                               preferred_element_type=F32)

    def merge(ti, maxes):
        ot = None
        for mp in range(2):
            m = functools.reduce(jnp.maximum, maxes[mp])
            w = [jnp.exp2(mc - m) for mc in maxes[mp]]
            den = functools.reduce(
                jnp.add, [wc * pv_scr[ti, mp, ci, VALUE_DIM:VALUE_DIM + 1, :] for ci, wc in enumerate(w)])
            norm = 1.0 / den if mp == 0 else -lam / den
            for ci, wc in enumerate(w):
                term = pv_scr[ti, mp, ci, :VALUE_DIM, :] * (wc * norm)
                ot = term if ot is None else ot + term
        ot = ot * lax.rsqrt(jnp.mean(ot * ot, axis=0, keepdims=True) + EPS)
        o_ref[0, ti * Q_TILE:(ti + 1) * Q_TILE, :] = (
            ot.T * (sg_ref[...] * (1.0 - lambda_init))).astype(BF16)

    pending = [scores(t) for t in range(SCORE_LOOKAHEAD)]
    maxes = ([], [])
    for t, (ti, ci, mp) in enumerate(tasks):
        if t + SCORE_LOOKAHEAD < len(tasks):
            pending.append(scores(t + SCORE_LOOKAHEAD))
        s = pending.pop(0)
        mc = jnp.max(_sublane_fold(s, jnp.max), axis=0, keepdims=True)
        e = jnp.exp2(s - mc).astype(BF16)
        maxes[mp].append(mc)
        src, lo, hi = chunks[ci]
        pv_scr[ti, mp, ci] = _mm(vt_refs[src][0, 0, :, lo:hi], e)
        if (ci, mp) == (len(chunks) - 1, 1):
            merge(ti, maxes)
            maxes = ([], [])


def _key_chunks(src, n_keys):
    assert n_keys % MXU_TILE == 0
    return tuple((src, lo, min(lo + KEY_CHUNK, n_keys)) for lo in range(0, n_keys, KEY_CHUNK))


def _cast_rows(n_rows, n_steps):
    rows = next(r for r in range(16, n_rows + 1, 16) if n_rows % r == 0 and r * n_steps >= n_rows)
    return rows, n_rows // rows


def _attn_call(lam, subln_g, q, k_lat, vt_lat, k_ctx, vt_ctx, lambda_init, weights):
    B, H, N, _ = q.shape
    n_ctx = k_ctx.shape[2]
    chunks = _key_chunks(0, N) + _key_chunks(1, n_ctx)
    tq = Q_TILE * Q_TILES_PER_STEP
    n_q = N // tq
    per_head = lambda b, h, i: (b, h, 0, 0)

    def cast_spec(w):
        rows, n_blocks = _cast_rows(w.shape[0], B * H * n_q)
        return pl.BlockSpec((rows, w.shape[1]),
                            lambda b, h, i: (jnp.minimum((b * H + h) * n_q + i, n_blocks - 1), 0))

    return pl.pallas_call(
        functools.partial(_attn_kernel, lambda_init=lambda_init, chunks=chunks),
        grid=(B, H, n_q),
        in_specs=[
            pl.BlockSpec((4, HEAD_DIM), lambda b, h, i: (0, 0)),
            pl.BlockSpec((1, VALUE_DIM), lambda b, h, i: (0, 0)),
            pl.BlockSpec((1, 1, tq, VALUE_DIM), lambda b, h, i: (b, h, i, 0)),
            pl.BlockSpec((1, 1, N, VALUE_DIM), per_head),
            pl.BlockSpec((1, 1, VT_ROWS, N), per_head),
            pl.BlockSpec((1, 1, n_ctx, VALUE_DIM), per_head),
            pl.BlockSpec((1, 1, VT_ROWS, n_ctx), per_head),
            *[cast_spec(w) for w in weights],
        ],
        out_specs=[pl.BlockSpec((1, tq, VALUE_DIM), lambda b, h, i: (b, i, h)),
                   *[cast_spec(w) for w in weights]],
        out_shape=[jax.ShapeDtypeStruct((B, N, D), BF16),
                   *[jax.ShapeDtypeStruct(w.shape, BF16) for w in weights]],
        scratch_shapes=[pltpu.VMEM((Q_TILES_PER_STEP, 2, len(chunks), VT_ROWS, Q_TILE), F32)],
        compiler_params=_cparams(("arbitrary", "arbitrary", "arbitrary")),
        name="attn",
    )(lam, subln_g.reshape(1, VALUE_DIM), q, k_lat, vt_lat, k_ctx, vt_ctx, *weights)


def _post_kernel(a_ref, w_ref, x_ref, gt_ref, g_ref, o_ref, *, pitched):
    if pitched:
        a = jnp.concatenate([a_ref[0, _slab(r), :] for r in range(a_ref.shape[1] // PITCH)], axis=0)
    else:
        a = a_ref[0]
    y = _mm(a.astype(BF16), w_ref[...])
    o_ref[0] = x_ref[0] + gt_ref[0] * _rms(y, g_ref[...])


def _post_call(a, w, x, gt, g, pitched=False):
    B, N, _ = x.shape
    tm = 512
    row = lambda b, m: (b, m, 0)
    return pl.pallas_call(
        functools.partial(_post_kernel, pitched=pitched),
        grid=(B, N // tm),
        in_specs=[
            pl.BlockSpec((1, tm // GRID_W * PITCH if pitched else tm, D), row),
            pl.BlockSpec((D, D), lambda b, m: (0, 0)),
            pl.BlockSpec((1, tm, D), row),
            pl.BlockSpec((1, 1, D), lambda b, m: (b, 0, 0)),
            pl.BlockSpec((1, D), lambda b, m: (0, 0)),
        ],
        out_specs=pl.BlockSpec((1, tm, D), row),
        out_shape=jax.ShapeDtypeStruct(x.shape, F32),
        compiler_params=_cparams(("parallel", "parallel")),
        name="post",
    )(a, w, x, gt, g)


def _ffn_kernel(x_ref, xn_ref, sh_ref, sc_ref, shn_ref, scn_ref, gt_ref, g_in_ref, g_out_ref,
                wg_ref, wu_ref, wd_ref, o_ref, h_scr, acc_scr, *, n_chunks):
    t, j = pl.program_id(0), pl.program_id(1)
    cur = lax.rem(t, 2)

    @pl.when((t == 0) & (j == 0))
    def _():
        h_scr[0] = _modulated_norm(x_ref[0], g_in_ref[...], sc_ref[0], sh_ref[0])

    @pl.when(j == 0)
    def _():
        acc_scr[...] = jnp.zeros_like(acc_scr)

    h = h_scr[cur]
    g = _mm(h, wg_ref[...])
    u = _mm(h, wu_ref[...])
    a = (g * jax.nn.sigmoid(g) * u).astype(BF16)
    acc_scr[...] += _mm(a, wd_ref[...])

    rows = _lookahead_rows(j, n_chunks, x_ref.shape[1])
    h_scr[1 - cur, rows, :] = _modulated_norm(xn_ref[0, rows, :], g_in_ref[...], scn_ref[0], shn_ref[0])

    @pl.when(j == n_chunks - 1)
    def _():
        o_ref[0] = x_ref[0] + gt_ref[0] * _rms(acc_scr[...], g_out_ref[...])


def _ffn_call(x, sh, sc, gt, g_in, g_out, w_gu, w_down, layer):
    B, N, _ = x.shape
    tm, th = 512, 512
    n_chunks = FFN // th
    tiles = N // tm
    n_tiles = B * tiles

    def nxt(t):
        return jnp.minimum(t + 1, n_tiles - 1)

    row = pl.BlockSpec((1, tm, D), lambda t, j: (t // tiles, t % tiles, 0))
    row_next = pl.BlockSpec((1, tm, D), lambda t, j: (nxt(t) // tiles, nxt(t) % tiles, 0))
    vec = pl.BlockSpec((1, 1, D), lambda t, j: (t // tiles, 0, 0))
    vec_next = pl.BlockSpec((1, 1, D), lambda t, j: (nxt(t) // tiles, 0, 0))
    gsp = pl.BlockSpec((1, D), lambda t, j: (0, 0))
    return pl.pallas_call(
        functools.partial(_ffn_kernel, n_chunks=n_chunks),
        grid=(n_tiles, n_chunks),
        in_specs=[
            row, row_next, vec, vec, vec_next, vec_next, vec, gsp, gsp,
            pl.BlockSpec((None, D, th), lambda t, j: (layer, 0, j)),
            pl.BlockSpec((None, D, th), lambda t, j: (layer, 0, j + n_chunks)),
            pl.BlockSpec((None, th, D), lambda t, j: (layer, j, 0)),
        ],
        out_specs=row,
        out_shape=jax.ShapeDtypeStruct(x.shape, F32),
        scratch_shapes=[pltpu.VMEM((2, tm, D), BF16), pltpu.VMEM((tm, D), F32)],
        compiler_params=_cparams(("arbitrary", "arbitrary")),
        name="ffn",
    )(x, x, sh, sc, sh, sc, gt, g_in, g_out, w_gu, w_gu, w_down)


PITCH = GRID_W + SUBLANES
N_PITCHED = GRID_W * PITCH


def _slab(i):
    return slice(i * PITCH, i * PITCH + GRID_W)


def _pad(i):
    return slice(i * PITCH + GRID_W, (i + 1) * PITCH)


def _complex_rows(re_ref, im_ref, col):
    rows = pl.ds(col, GRID_W, stride=PITCH)
    return jnp.concatenate([re_ref[0, rows, :].astype(BF16), im_ref[0, rows, :].astype(BF16)], axis=0)


def _fin_kernel(x_ref, xn_ref, sh_ref, sc_ref, shn_ref, scn_ref, g_ref, w_ref, u_ref, h_scr):
    t = pl.program_id(0)
    cur = lax.rem(t, 2)

    @pl.when(t == 0)
    def _():
        h_scr[0] = _modulated_norm(x_ref[0], g_ref[...], sc_ref[0], sh_ref[0])

    u = _mm(h_scr[cur], w_ref[...])
    for r in range(x_ref.shape[1] // GRID_W):
        u_ref[0, _slab(r), :] = u[r * GRID_W:(r + 1) * GRID_W]
        u_ref[0, _pad(r), :] = jnp.zeros((PITCH - GRID_W, D), F32)
    h_scr[1 - cur] = _modulated_norm(xn_ref[0], g_ref[...], scn_ref[0], shn_ref[0])


def _fin_call(x, sh, sc, g, w_in):
    B, N, _ = x.shape
    tm = 512
    tiles = N // tm
    nxt = _next_tile(B * tiles)
    row = pl.BlockSpec((1, tm, D), lambda t: (t // tiles, t % tiles, 0))
    row_next = pl.BlockSpec((1, tm, D), lambda t: (nxt(t) // tiles, nxt(t) % tiles, 0))
    vec = pl.BlockSpec((1, 1, D), lambda t: (t // tiles, 0, 0))
    vec_next = pl.BlockSpec((1, 1, D), lambda t: (nxt(t) // tiles, 0, 0))
    return pl.pallas_call(
        _fin_kernel,
        grid=(B * tiles,),
        in_specs=[
            row, row_next, vec, vec, vec_next, vec_next,
            pl.BlockSpec((1, D), lambda t: (0, 0)),
            pl.BlockSpec((D, D), lambda t: (0, 0)),
        ],
        out_specs=pl.BlockSpec((1, tm // GRID_W * PITCH, D), lambda t: (t // tiles, t % tiles, 0)),
        out_shape=jax.ShapeDtypeStruct((B, N_PITCHED, D), F32),
        scratch_shapes=[pltpu.VMEM((2, tm, D), BF16)],
        compiler_params=_cparams(("arbitrary",)),
        name="fin",
    )(x, x, sh, sc, sh, sc, g, w_in)


def _dft_a_kernel(ulo_ref, uhi_ref, cs_ref, m_ref, twc_ref, tws_ref, ar_ref, ai_ref):
    cs, m = cs_ref[...], m_ref[...]
    zero_pad = jnp.zeros((PITCH - GRID_W, GROUP_DIM), F32)
    batch = MXU_TILE // GRID_W
    for col0 in range(0, GRID_W, batch):
        u = []
        for col in range(col0, col0 + batch):
            rows = pl.ds(col, GRID_W, stride=PITCH)
            u.append(jnp.concatenate([ulo_ref[0, rows, :], uhi_ref[0, rows, :]], axis=1).astype(BF16))
        zb = _mm(jnp.concatenate(u, axis=0), cs)
        for i, col in enumerate(range(col0, col0 + batch)):
            z = zb[i * GRID_W:(i + 1) * GRID_W]
            z = jnp.concatenate([z[:, :GROUP_DIM], z[:, GROUP_DIM:]], axis=0).astype(BF16)
            a = _mm(m, z)
            ar, ai = a[:GRID_W], a[GRID_W:]
            c = jnp.concatenate([twc_ref[col]] * (GROUP_DIM // LANES), axis=1)
            s = jnp.concatenate([tws_ref[col]] * (GROUP_DIM // LANES), axis=1)
            ar_ref[0, _slab(col), :] = ar * c + ai * s
            ai_ref[0, _slab(col), :] = ai * c - ar * s
            ar_ref[0, _pad(col), :] = zero_pad
            ai_ref[0, _pad(col), :] = zero_pad


def _dft_a_call(u, cs, m_a, twc, tws):
    B = u.shape[0]
    halves = GROUP_DIM // LANES
    assert halves == 2
    lane_blk = lambda h: pl.BlockSpec((1, N_PITCHED, LANES), lambda b, i: (b, 0, halves * i + h))
    out = pl.BlockSpec((1, N_PITCHED, GROUP_DIM), lambda b, i: (b, 0, i))
    tw = pl.BlockSpec((GRID_W, GRID_W, LANES), lambda b, i: (0, 0, 0))
    shp = jax.ShapeDtypeStruct((B, N_PITCHED, D), F32)
    return pl.pallas_call(
        _dft_a_kernel,
        grid=(B, GROUPS),
        in_specs=[
            lane_blk(0), lane_blk(1),
            pl.BlockSpec((GROUP_DIM, 2 * GROUP_DIM), lambda b, i: (0, 0)),
            pl.BlockSpec((2 * GRID_W, 2 * GRID_W), lambda b, i: (0, 0)),
            tw, tw,
        ],
        out_specs=[out, out],
        out_shape=[shp, shp],
        compiler_params=_cparams(("parallel", "parallel")),
        name="dft_a",
    )(u, u, cs, m_a, twc, tws)


def _dft_c_kernel(ar_ref, ai_ref, m_ref, f_ref):
    m = m_ref[...]
    for kr in range(GRID_W):
        f = _mm(m, _complex_rows(ar_ref, ai_ref, kr))
        f_ref[0, pl.ds(kr, GRID_W, stride=PITCH), :] = f
    for j in range(GRID_W, PITCH):
        f_ref[0, pl.ds(j, GRID_W, stride=PITCH), :] = jnp.zeros((GRID_W, f_ref.shape[2]), F32)


def _dft_c_call(ar, ai, m_c):
    B = ar.shape[0]
    blk = pl.BlockSpec((1, N_PITCHED, LANES), lambda b, i: (b, 0, i))
    return pl.pallas_call(
        _dft_c_kernel,
        grid=(B, D // LANES),
        in_specs=[blk, blk, pl.BlockSpec((GRID_W, 2 * GRID_W), lambda b, i: (0, 0))],
        out_specs=blk,
        out_shape=jax.ShapeDtypeStruct((B, N_PITCHED, D), F32),
        compiler_params=_cparams(("parallel", "parallel")),
        name="dft_c",
    )(ar, ai, m_c)


def _rope_tables(n_tokens):
    pos = jnp.arange(n_tokens)
    lane = jnp.arange(LANES)
    d = lane % HEAD_DIM
    axis = d // (2 * ROPE_FREQS)
    second_half = (d % (2 * ROPE_FREQS)) >= ROPE_FREQS
    inv_freq = ROPE_THETA ** (-(d % ROPE_FREQS).astype(F32) / ROPE_FREQS)
    coord = jnp.where(axis[None, :] == 0, (pos // GRID_W)[:, None], (pos % GRID_W)[:, None])
    ang = coord.astype(F32) * inv_freq[None, :]
    c, s = jnp.cos(ang), jnp.sin(ang)
    return c, jnp.where(second_half[None, :], 0.0, -s), jnp.where(second_half[None, :], s, 0.0)


def _dft_tables():
    two_pi = 2.0 * math.pi
    ch = jnp.arange(GROUP_DIM)
    ang_c = two_pi * ((ch[:, None] * ch[None, :]) % GROUP_DIM).astype(F32) / GROUP_DIM
    ch_scale = GROUP_DIM ** -0.5
    cs = jnp.concatenate([jnp.cos(ang_c), -jnp.sin(ang_c)], axis=1) * ch_scale
    r = jnp.arange(GRID_W)
    ang_r = two_pi * ((r[:, None] * r[None, :]) % GRID_W).astype(F32) / GRID_W
    cr, sr = jnp.cos(ang_r) / 8.0, jnp.sin(ang_r) / 8.0
    m_a = jnp.concatenate([jnp.concatenate([cr, sr], axis=1),
                           jnp.concatenate([-sr, cr], axis=1)], axis=0)
    m_c = jnp.concatenate([cr, sr], axis=1)
    ang_t = two_pi * (r[:, None] * r[None, :]).astype(F32) / (GRID_W * GRID_W)
    twc = jnp.broadcast_to(jnp.cos(ang_t)[:, :, None], (GRID_W, GRID_W, LANES))
    tws = jnp.broadcast_to(jnp.sin(ang_t)[:, :, None], (GRID_W, GRID_W, LANES))
    return cs.astype(BF16), m_a.astype(BF16), m_c.astype(BF16), twc, tws


def kernel(x, c, ctx, c_ctx, mod_w, mod_b, norm_g, ffn_w_gu, ffn_w_down, attn_w_qkv, attn_w_o,
           attn_lambda, attn_subln_g, four_w_in, four_w_out):
    B, N, _ = x.shape
    Lc = ctx.shape[1]
    assert x.shape == (B, GRID_W * GRID_W, D) and ctx.shape == (B, Lc, D) and N % Lc == 0

    cond = jnp.concatenate([c, c_ctx[None, :], jnp.zeros((8 - B - 1, D), F32)], axis=0)
    mods = _ada_call(cond, mod_w, mod_b)

    def lat(layer, idx):
        return mods[layer, :B, idx * D:(idx + 1) * D].reshape(B, 1, D)

    def ctx_mod(layer, idx):
        return mods[layer, B:B + 1, idx * D:(idx + 1) * D]

    g = norm_g.reshape(norm_g.shape[0], 4, 1, D)

    lambda_init = 0.8 - 0.6 * math.exp(-0.3 * 0)
    rope_c, rope_slo, rope_shi = _rope_tables(N)
    w_qkv = attn_w_qkv[0].astype(BF16)
    q, k_lat, vt_lat = _qkv_call(x, lat(0, 0), lat(0, 1), g[0, 0], w_qkv, rope_c, rope_slo, rope_shi)
    k_ctx, vt_ctx = _kv_ctx_call(ctx, ctx_mod(0, 0), ctx_mod(0, 1), g[0, 0], w_qkv)
    later = [ffn_w_gu.reshape(-1, 2 * FFN), ffn_w_down.reshape(-1, D), attn_w_o[0], four_w_in[0], four_w_out[0]]
    o, w_gu, w_down, w_o, w_in, w_out = _attn_call(
        attn_lambda[0], attn_subln_g[0], q, k_lat, vt_lat, k_ctx, vt_ctx, lambda_init, later)
    w_gu, w_down = w_gu.reshape(ffn_w_gu.shape), w_down.reshape(ffn_w_down.shape)
    x = _post_call(o, w_o, x, lat(0, 2), g[0, 1])
    x = _ffn_call(x, lat(0, 3), lat(0, 4), lat(0, 5), g[0, 2], g[0, 3], w_gu, w_down, 0)

    cs, m_a, m_c, twc, tws = _dft_tables()
    u = _fin_call(x, lat(1, 0), lat(1, 1), g[1, 0], w_in)
    f = _dft_c_call(*_dft_a_call(u, cs, m_a, twc, tws), m_c)
    x = _post_call(f, w_out, x, lat(1, 2), g[1, 1], pitched=True)
    x = _ffn_call(x, lat(1, 3), lat(1, 4), lat(1, 5), g[1, 2], g[1, 3], w_gu, w_down, 1)
    return x
```
